```python
import math
import numpy as np
import jax
import jax.numpy as jnp
from jax import lax

D_MODEL = 1024
BATCH = 4
SEQ = 8192
DEPTH = 4

GRID_W = 64
CTX_LEN = 256
N_DIRS = 2
EPS = 1e-6
S5_WIDTH = D_MODEL // 2
S5_GROUP = 16
S5_GROUPS = S5_WIDTH // S5_GROUP
S5_STATE = 64
DT_MIN = 1e-3
DT_MAX = 1e-1
GLA_HEADS = 4
GLA_DV = D_MODEL // 2 // GLA_HEADS
GLA_DK = GLA_DV // 2
GLA_WIDTH = GLA_HEADS * GLA_DV
GLA_KEY = GLA_HEADS * GLA_DK
GLA_RANK = 16
GLA_NORMALIZER = 16.0
GLA_CHUNK = 64
MIX_WIDTH = S5_WIDTH + GLA_WIDTH
IN_WIDTH = 2 * S5_WIDTH + 2 * GLA_KEY + 2 * GLA_WIDTH + N_DIRS * GLA_RANK

kernel_name = 'hybrid_s5_gla_prefix_dit'


def _split_points():
    sizes = (S5_WIDTH, S5_WIDTH, GLA_KEY, GLA_KEY, GLA_WIDTH, GLA_WIDTH, N_DIRS * GLA_RANK)
    return [int(v) for v in np.cumsum(sizes)[:-1]]


def rms_norm(x, gain):
    xf = x.astype(jnp.float32)
    y = xf * lax.rsqrt(jnp.mean(xf * xf, axis=-1, keepdims=True) + EPS)
    return (y * gain.astype(jnp.float32)).astype(x.dtype)


def modulation(cond, w_mod, b_mod):
    m = jax.nn.silu(cond) @ w_mod + b_mod
    return jnp.split(m, 3, axis=-1)


def _flip(t, rev):
    return t[:, ::-1] if rev else t


def to_colmajor(t, rows):
    b, l = t.shape[:2]
    return t.reshape(b, rows, GRID_W, *t.shape[2:]).swapaxes(1, 2).reshape(b, l, *t.shape[2:])


def from_colmajor(t, rows):
    b, l = t.shape[:2]
    return t.reshape(b, GRID_W, rows, *t.shape[2:]).swapaxes(1, 2).reshape(b, l, *t.shape[2:])


def _lin_combine(e1, e2):
    a1, b1 = e1
    a2, b2 = e2
    return a2 * a1, a2 * b1 + b2


def s5_scan(lam_bar, bu, h0):
    a = jnp.broadcast_to(lam_bar, bu.shape)
    a_cum, h = lax.associative_scan(_lin_combine, (a, bu), axis=1)
    if h0 is None:
        return h
    return h + a_cum * h0[:, None]


def s5_discretize(lam_re, lam_im, log_dt, b_cplx):
    lam = lax.complex(lam_re.astype(jnp.float32), lam_im.astype(jnp.float32))
    dt = jnp.exp(log_dt.astype(jnp.float32))[:, None]
    lam_bar = jnp.exp(lam * dt)
    b_bar = ((lam_bar - 1.0) / lam)[..., None] * b_cplx
    return lam_bar, b_bar


def s5_branch(u_c, u_l, lam_re, lam_im, log_dt, b_re, b_im, c_re, c_im, d_skip, w_glu, b_glu, need_ctx):
    dtype = u_l.dtype
    grp = lambda u: u.astype(jnp.float32).reshape(*u.shape[:2], S5_GROUPS, S5_GROUP)
    uc, ul = grp(u_c), grp(u_l)
    b_cplx = lax.complex(b_re.astype(jnp.float32), b_im.astype(jnp.float32))
    h_c = 0.0
    h_l = 0.0
    for d in range(N_DIRS):
        rev = d == 1
        lam_bar, b_bar = s5_discretize(lam_re[d], lam_im[d], log_dt[d], b_cplx)
        hc = s5_scan(lam_bar, jnp.einsum('gnp,blgp->blgn', b_bar, _flip(uc, rev)), None)
        hl = s5_scan(lam_bar, jnp.einsum('gnp,blgp->blgn', b_bar, _flip(ul, rev)), hc[:, -1])
        h_c = h_c + _flip(hc, rev)
        h_l = h_l + _flip(hl, rev)
    c_cplx = lax.complex(c_re.astype(jnp.float32), c_im.astype(jnp.float32))
    d_g = d_skip.astype(jnp.float32).reshape(S5_GROUPS, S5_GROUP)

    def readout(h, u):
        y = jnp.einsum('gpn,blgn->blgp', c_cplx, h).real + d_g * u
        y = jax.nn.gelu(y.reshape(*u.shape[:2], S5_WIDTH)).astype(dtype)
        return y * jax.nn.sigmoid(y @ w_glu + b_glu)

    y_c = readout(h_c, uc) if need_ctx else None
    return y_c, readout(h_l, ul)


def gla_chunked(q, k, v, g, s0):
    bsz, l, h, dk = q.shape
    dv = v.shape[-1]
    n = l // GLA_CHUNK
    chunks = lambda t: t.astype(jnp.float32).reshape(bsz, n, GLA_CHUNK, h, t.shape[-1])
    q, k, v, g = chunks(q), chunks(k), chunks(v), chunks(g)
    gc = jnp.cumsum(g, axis=2)
    g_last = gc[:, :, -1]
    q_t = q * jnp.exp(gc)
    k_t = k * jnp.exp(-gc)
    scores = jnp.einsum('bnihd,bnjhd->bnhij', q_t, k_t)
    mask = jnp.tril(jnp.ones((GLA_CHUNK, GLA_CHUNK), dtype=bool))
    scores = jnp.where(mask, scores, 0.0)
    o = jnp.einsum('bnhij,bnjhv->bnihv', scores, v)
    ds = jnp.einsum('bnjhd,bnjhv->bnhdv', k * jnp.exp(g_last[:, :, None] - gc), v)
    decay = jnp.exp(g_last)
    if s0 is None:
        s0 = jnp.zeros((bsz, h, dk, dv), jnp.float32)

    def step(s, inp):
        dec, d_s = inp
        return dec[..., None] * s + d_s, s

    s_final, s_in = lax.scan(step, s0, (jnp.moveaxis(decay, 1, 0), jnp.moveaxis(ds, 1, 0)))
    s_in = jnp.moveaxis(s_in, 0, 1)
    o = o + jnp.einsum('bnihd,bnhdv->bnihv', q_t, s_in)
    return o.reshape(bsz, l, h, dv), s_final


def gla_log_decay(lr, w_gate, b_gate, d):
    z = lr[..., d * GLA_RANK:(d + 1) * GLA_RANK] @ w_gate[d] + b_gate[d]
    g = jax.nn.log_sigmoid(z.astype(jnp.float32)) / GLA_NORMALIZER
    return g.reshape(*g.shape[:2], GLA_HEADS, GLA_DK)


def gla_branch(q_c, k_c, v_c, lr_c, q_l, k_l, v_l, lr_l, w_gate, b_gate, norm_g, need_ctx):
    dtype = q_l.dtype
    heads = lambda t, dd: t.reshape(*t.shape[:2], GLA_HEADS, dd)
    q_c, q_l = heads(q_c, GLA_DK) * GLA_DK ** -0.5, heads(q_l, GLA_DK) * GLA_DK ** -0.5
    k_c, k_l = heads(k_c, GLA_DK), heads(k_l, GLA_DK)
    v_c, v_l = heads(v_c, GLA_DV), heads(v_l, GLA_DV)
    o_c = 0.0
    o_l = 0.0
    for d in range(N_DIRS):
        rev = d == 1
        g_c = gla_log_decay(lr_c, w_gate, b_gate, d)
        g_l = gla_log_decay(lr_l, w_gate, b_gate, d)
        oc, s_c = gla_chunked(_flip(q_c, rev), _flip(k_c, rev), _flip(v_c, rev), _flip(g_c, rev), None)
        ol, _ = gla_chunked(_flip(q_l, rev), _flip(k_l, rev), _flip(v_l, rev), _flip(g_l, rev), s_c)
        o_c = o_c + _flip(oc, rev)
        o_l = o_l + _flip(ol, rev)

    def finish(o):
        return rms_norm(o, norm_g).reshape(*o.shape[:2], GLA_WIDTH).astype(dtype)

    y_c = finish(o_c) if need_ctx else None
    return y_c, finish(o_l)


def hybrid_layer(x_lat, x_ctx, c, c_ctx, norm_g, w_mod, b_mod, w_in, lam_re, lam_im, log_dt, b_re, b_im,
                 c_re, c_im, d_skip, w_glu, b_glu, gla_w_gate, gla_b_gate, gla_norm_g, w_out, need_ctx_out):
    rows = x_lat.shape[1] // GRID_W
    sh_l, sc_l, gt_l = modulation(c, w_mod, b_mod)
    sh_c, sc_c, gt_c = modulation(c_ctx, w_mod, b_mod)
    h_l = rms_norm(x_lat, norm_g) * (1.0 + sc_l[:, None]) + sh_l[:, None]
    h_c = rms_norm(x_ctx, norm_g) * (1.0 + sc_c) + sh_c
    sp = _split_points()
    u_l, zs_l, q_l, k_l, v_l, zg_l, lr_l = jnp.split(h_l @ w_in, sp, axis=-1)
    u_c, zs_c, q_c, k_c, v_c, zg_c, lr_c = jnp.split(h_c @ w_in, sp, axis=-1)

    ys_c, ys_l = s5_branch(u_c, u_l, lam_re, lam_im, log_dt, b_re, b_im, c_re, c_im, d_skip, w_glu, b_glu,
                           need_ctx_out)
    cm = lambda t: to_colmajor(t, rows)
    yg_c, yg_l = gla_branch(q_c, k_c, v_c, lr_c, cm(q_l), cm(k_l), cm(v_l), cm(lr_l),
                            gla_w_gate, gla_b_gate, gla_norm_g, need_ctx_out)
    yg_l = from_colmajor(yg_l, rows)

    y_l = jnp.concatenate([ys_l * jax.nn.silu(zs_l), yg_l * jax.nn.silu(zg_l)], axis=-1) @ w_out
    x_lat = x_lat + gt_l[:, None] * y_l
    if need_ctx_out:
        y_c = jnp.concatenate([ys_c * jax.nn.silu(zs_c), yg_c * jax.nn.silu(zg_c)], axis=-1) @ w_out
        x_ctx = x_ctx + gt_c * y_c
    return x_lat, x_ctx


def setup_inputs(seed: int = 0) -> dict:
    key = jax.random.key(seed)
    ks = jax.random.split(key, 24)
    f32 = jnp.float32
    nrm = lambda k, shape, s: s * jax.random.normal(k, shape, f32)
    return {
        'x': nrm(ks[0], (BATCH, SEQ, D_MODEL), 1.0),
        'c': nrm(ks[1], (BATCH, D_MODEL), 1.0),
        'ctx': nrm(ks[2], (BATCH, CTX_LEN, D_MODEL), 1.0),
        'c_ctx': nrm(ks[3], (D_MODEL,), 1.0),
        'norm_g': 1.0 + nrm(ks[4], (DEPTH, D_MODEL), 0.02),
        'w_mod': nrm(ks[5], (DEPTH, D_MODEL, 3 * D_MODEL), 0.5 * D_MODEL ** -0.5),
        'b_mod': nrm(ks[6], (DEPTH, 3 * D_MODEL), 0.02),
        'w_in': nrm(ks[7], (DEPTH, D_MODEL, IN_WIDTH), D_MODEL ** -0.5),
        's5_lam_re': jnp.full((DEPTH, N_DIRS, S5_GROUPS, S5_STATE), -0.5, f32),
        's5_lam_im': jnp.broadcast_to(math.pi * jnp.arange(S5_STATE, dtype=f32),
                                      (DEPTH, N_DIRS, S5_GROUPS, S5_STATE)),
        's5_log_dt': jax.random.uniform(ks[8], (DEPTH, N_DIRS, S5_GROUPS), f32,
                                        math.log(DT_MIN), math.log(DT_MAX)),
        's5_b_re': nrm(ks[9], (DEPTH, S5_GROUPS, S5_STATE, S5_GROUP), (2 * S5_GROUP) ** -0.5),
        's5_b_im': nrm(ks[10], (DEPTH, S5_GROUPS, S5_STATE, S5_GROUP), (2 * S5_GROUP) ** -0.5),
        's5_c_re': nrm(ks[11], (DEPTH, S5_GROUPS, S5_GROUP, S5_STATE), S5_STATE ** -0.5),
        's5_c_im': nrm(ks[12], (DEPTH, S5_GROUPS, S5_GROUP, S5_STATE), S5_STATE ** -0.5),
        's5_d': nrm(ks[13], (DEPTH, S5_WIDTH), 1.0),
        's5_w_glu': nrm(ks[14], (DEPTH, S5_WIDTH, S5_WIDTH), S5_WIDTH ** -0.5),
        's5_b_glu': nrm(ks[15], (DEPTH, S5_WIDTH), 0.02),
        'gla_w_gate': nrm(ks[16], (DEPTH, N_DIRS, GLA_RANK, GLA_KEY), GLA_RANK ** -0.5),
        'gla_b_gate': nrm(ks[17], (DEPTH, N_DIRS, GLA_KEY), 0.1),
        'gla_norm_g': 1.0 + nrm(ks[18], (DEPTH, GLA_DV), 0.02),
        'w_out': nrm(ks[19], (DEPTH, MIX_WIDTH, D_MODEL), MIX_WIDTH ** -0.5),
        'final_norm': 1.0 + nrm(ks[20], (D_MODEL,), 0.02),
    }


def reference(x, c, ctx, c_ctx, norm_g, w_mod, b_mod, w_in, s5_lam_re, s5_lam_im, s5_log_dt, s5_b_re, s5_b_im,
              s5_c_re, s5_c_im, s5_d, s5_w_glu, s5_b_glu, gla_w_gate, gla_b_gate, gla_norm_g, w_out, final_norm):
    x_lat, x_ctx = x, ctx
    for i in range(DEPTH):
        x_lat, x_ctx = hybrid_layer(
            x_lat, x_ctx, c, c_ctx, norm_g[i], w_mod[i], b_mod[i], w_in[i],
            s5_lam_re[i], s5_lam_im[i], s5_log_dt[i], s5_b_re[i], s5_b_im[i], s5_c_re[i], s5_c_im[i],
            s5_d[i], s5_w_glu[i], s5_b_glu[i], gla_w_gate[i], gla_b_gate[i], gla_norm_g[i], w_out[i],
            need_ctx_out=(i < DEPTH - 1))
    return rms_norm(x_lat, final_norm)
```

```python
import functools
import math

import jax
import jax.numpy as jnp
from jax import lax
from jax.experimental import pallas as pl
from jax.experimental.pallas import tpu as pltpu

F32 = jnp.float32
BF16 = jnp.bfloat16

D_MODEL = 1024
DEPTH = 4
GRID_W = 64
EPS = 1e-6
S5_WIDTH = 512
S5_GROUP = 16
S5_GROUPS = 32
S5_STATE = 64
S5_CHUNK = 16
S5_ROW = S5_CHUNK * S5_GROUP
S5_SLANES = 4 * S5_STATE
GLA_HEADS = 4
GLA_DV = 128
GLA_DK = 64
GLA_WIDTH = 512
GLA_KEY = 256
GLA_RANK = 16
GLA_NORMALIZER = 16.0
GLA_CHUNK = 64
GLA_COL = 128
LR_PAD = 128
IN_PAD = 2 * S5_WIDTH + 2 * GLA_KEY + 2 * GLA_WIDTH + LR_PAD

ROW_BLOCK = 512
VMEM_LIMIT = 48 * 1024 * 1024


def _cparams(*sem):
    return pltpu.CompilerParams(dimension_semantics=sem, vmem_limit_bytes=VMEM_LIMIT)


def _mod_kernel(cond_ref, w_ref, b_ref, o_ref):
    cnd = cond_ref[...]
    a = cnd * jax.nn.sigmoid(cnd)
    o_ref[...] = jnp.dot(a, w_ref[...], preferred_element_type=F32,
                         precision=lax.Precision.HIGHEST) + b_ref[...]


def _modulation(cond8, w_mod, b_mod):
    nblk = 3
    return pl.pallas_call(
        _mod_kernel,
        grid=(DEPTH, nblk),
        in_specs=[
            pl.BlockSpec((8, D_MODEL), lambda l, j: (0, 0)),
            pl.BlockSpec((None, D_MODEL, D_MODEL), lambda l, j: (l, 0, j)),
            pl.BlockSpec((None, 1, D_MODEL), lambda l, j: (l, 0, j)),
        ],
        out_specs=pl.BlockSpec((None, 8, D_MODEL), lambda l, j: (l, 0, j)),
        out_shape=jax.ShapeDtypeStruct((DEPTH, 8, 3 * D_MODEL), F32),
        compiler_params=_cparams("arbitrary", "arbitrary"),
        name="modulation",
    )(cond8, w_mod, b_mod.reshape(DEPTH, 1, 3 * D_MODEL))


_IN_SEGS = (("u", S5_WIDTH), ("zs", S5_WIDTH), ("q", GLA_KEY), ("k", GLA_KEY), ("v", GLA_WIDTH),
            ("zg", GLA_WIDTH), ("lr", LR_PAD))


def _inproj_kernel(x_ref, mod_ref, gain_ref, w_ref, *out_refs):
    x = x_ref[...]
    ms = jnp.mean(x * x, axis=-1, keepdims=True)
    y = x * lax.rsqrt(ms + EPS) * gain_ref[...]
    shift = mod_ref[:, 0:D_MODEL]
    scale = mod_ref[:, D_MODEL:2 * D_MODEL]
    h = (y * (1.0 + scale) + shift).astype(BF16)
    off = 0
    for (_, width), o_ref in zip(_IN_SEGS, out_refs):
        o_ref[...] = jnp.dot(h, w_ref[:, off:off + width], preferred_element_type=F32)
        off += width


def _inproj(x2, mods, gain, w_pad, mod_row):
    rows = x2.shape[0]
    tb = min(ROW_BLOCK, rows)
    grid = (rows // tb,)
    return pl.pallas_call(
        _inproj_kernel,
        grid=grid,
        in_specs=[
            pl.BlockSpec((tb, D_MODEL), lambda i: (i, 0)),
            pl.BlockSpec((None, 1, 3 * D_MODEL), lambda i: (mod_row(i), 0, 0)),
            pl.BlockSpec((1, D_MODEL), lambda i: (0, 0)),
            pl.BlockSpec((D_MODEL, IN_PAD), lambda i: (0, 0)),
        ],
        out_specs=[pl.BlockSpec((tb, w), lambda i: (i, 0)) for _, w in _IN_SEGS],
        out_shape=[jax.ShapeDtypeStruct((rows, w), F32) for _, w in _IN_SEGS],
        compiler_params=_cparams("arbitrary"),
        name="inproj",
    )(x2, mods, gain, w_pad)


def _s5_matrices(lam_re, lam_im, log_dt, b_re, b_im, c_re, c_im, d_skip):
    hp = lax.Precision.HIGHEST
    T = S5_CHUNK
    lam = lax.complex(lam_re.astype(F32), lam_im.astype(F32))
    dt = jnp.exp(log_dt.astype(F32))[..., None]
    ldt = lam * dt
    lam_bar = jnp.exp(ldt)
    b_c = lax.complex(b_re.astype(F32), b_im.astype(F32))
    b_bar = ((lam_bar - 1.0) / lam)[..., None] * b_c[None]
    c_c = lax.complex(c_re.astype(F32), c_im.astype(F32))
    steps = jnp.arange(T + 1, dtype=F32)
    pw = jnp.exp(ldt[:, None] * steps[None, :, None, None].astype(jnp.complex64))

    kern = jnp.einsum('gpn,dtgn,dgnq->dgtpq', c_c, pw[:, :T], b_bar, precision=hp).real
    ii = jnp.arange(T)[:, None]
    jj = jnp.arange(T)[None, :]
    lag_f = jnp.clip(jj - ii, 0, T - 1)
    lag_b = jnp.clip(ii - jj, 0, T - 1)
    kf = jnp.where((jj >= ii)[None, :, :, None, None], kern[0][:, lag_f], 0.0)
    kb = jnp.where((ii >= jj)[None, :, :, None, None], kern[1][:, lag_b], 0.0)
    eye_t = jnp.eye(T, dtype=F32)[None, :, :, None, None]
    eye_p = jnp.eye(S5_GROUP, dtype=F32)[None, None, None]
    d_g = d_skip.astype(F32).reshape(S5_GROUPS, 1, 1, S5_GROUP, 1)
    m5 = kf + kb + eye_t * eye_p * d_g
    m_toep = m5.transpose(0, 1, 4, 2, 3).reshape(S5_GROUPS, S5_ROW, S5_ROW)

    pf = pw[0][1:T + 1]
    pb = pw[1][::-1][:T]

    def readout(pj):
        wc = c_c[:, None] * pj.transpose(1, 0, 2)[:, :, None, :]
        wc = wc.transpose(0, 3, 1, 2).reshape(S5_GROUPS, S5_STATE, S5_ROW)
        return jnp.concatenate([wc.real, -wc.imag], axis=1)

    c_mat = jnp.concatenate([readout(pf), readout(pb)], axis=1)

    qf = pw[0][:T][::-1]
    qb = pw[1][:T]

    def inject(qi, bb):
        wb = qi.transpose(1, 0, 2)[..., None] * bb[:, None]
        wb = wb.transpose(0, 1, 3, 2).reshape(S5_GROUPS, S5_ROW, S5_STATE)
        return jnp.concatenate([wb.real, wb.imag], axis=2)

    b_mat = jnp.concatenate([inject(qf, b_bar[0]), inject(qb, b_bar[1])], axis=2)

    a_chunk = pw[:, T]
    a1 = jnp.concatenate([a_chunk.real, a_chunk.real], axis=-1)
    a2 = jnp.concatenate([-a_chunk.imag, a_chunk.imag], axis=-1)
    return m_toep.astype(BF16), c_mat.astype(BF16), b_mat.astype(BF16), a1, a2


def _s5_inject_kernel(x_ref, b_ref, s_ref):
    s_ref[...] = jnp.dot(x_ref[...].astype(BF16), b_ref[...], preferred_element_type=F32)


def _s5_inject(xg, b_mat):
    bsz, _, nchunk, _ = xg.shape
    return pl.pallas_call(
        _s5_inject_kernel,
        grid=(bsz, S5_GROUPS),
        in_specs=[
            pl.BlockSpec((None, None, nchunk, S5_ROW), lambda b, g: (b, g, 0, 0)),
            pl.BlockSpec((None, S5_ROW, S5_SLANES), lambda b, g: (g, 0, 0)),
        ],
        out_specs=pl.BlockSpec((None, None, nchunk, S5_SLANES), lambda b, g: (b, g, 0, 0)),
        out_shape=jax.ShapeDtypeStruct((bsz, S5_GROUPS, nchunk, S5_SLANES), F32),
        compiler_params=_cparams("arbitrary", "arbitrary"),
        name="s5_inject",
    )(xg, b_mat)


S5_SCAN_BLOCK = 16


def _s5_scan_kernel(sf_ref, sb_ref, a1_ref, a2_ref, hf_ref, hb_ref, hf_scr, hb_scr):
    @pl.when(pl.program_id(0) == 0)
    def _():
        hf_scr[...] = jnp.zeros_like(hf_scr)
        hb_scr[...] = jnp.zeros_like(hb_scr)

    a1f = a1_ref[0]
    a2f = a2_ref[0]
    a1b = a1_ref[1]
    a2b = a2_ref[1]
    hf = hf_scr[...]
    hb = hb_scr[...]
    for c in range(S5_SCAN_BLOCK):
        hf_ref[c] = hf
        hf = a1f * hf + a2f * pltpu.roll(hf, S5_STATE, 1) + sf_ref[c]
        cb = S5_SCAN_BLOCK - 1 - c
        hb_ref[cb] = hb
        hb = a1b * hb + a2b * pltpu.roll(hb, S5_STATE, 1) + sb_ref[cb]
    hf_scr[...] = hf
    hb_scr[...] = hb


def _s5_scan(s_all, a1, a2):
    ctot, rows, _ = s_all.shape
    nblk = ctot // S5_SCAN_BLOCK
    half = 2 * S5_STATE

    def bwd_block(i):
        return jnp.where(i == 0, 0, nblk - i)

    return pl.pallas_call(
        _s5_scan_kernel,
        grid=(nblk,),
        in_specs=[
            pl.BlockSpec((S5_SCAN_BLOCK, rows, half), lambda i: (i, 0, 0)),
            pl.BlockSpec((S5_SCAN_BLOCK, rows, half), lambda i: (bwd_block(i), 0, 1)),
            pl.BlockSpec((2, rows, half), lambda i: (0, 0, 0)),
            pl.BlockSpec((2, rows, half), lambda i: (0, 0, 0)),
        ],
        out_specs=[
            pl.BlockSpec((S5_SCAN_BLOCK, rows, half), lambda i: (i, 0, 0)),
            pl.BlockSpec((S5_SCAN_BLOCK, rows, half), lambda i: (bwd_block(i), 0, 0)),
        ],
        out_shape=[jax.ShapeDtypeStruct((ctot, rows, half), F32)] * 2,
        scratch_shapes=[pltpu.VMEM((rows, half), F32), pltpu.VMEM((rows, half), F32)],
        compiler_params=_cparams("arbitrary"),
        name="s5_scan",
    )(s_all, s_all, a1, a2)


def _s5_output_kernel(x_ref, h_ref, m_ref, c_ref, y_ref):
    y = jnp.dot(x_ref[...].astype(BF16), m_ref[...], preferred_element_type=F32)
    y_ref[...] = y + jnp.dot(h_ref[...].astype(BF16), c_ref[...], preferred_element_type=F32)


def _s5_output(xg, hg, m_toep, c_mat):
    bsz, _, nchunk, _ = xg.shape
    return pl.pallas_call(
        _s5_output_kernel,
        grid=(bsz, S5_GROUPS),
        in_specs=[
            pl.BlockSpec((None, None, nchunk, S5_ROW), lambda b, g: (b, g, 0, 0)),
            pl.BlockSpec((None, None, nchunk, S5_SLANES), lambda b, g: (b, g, 0, 0)),
            pl.BlockSpec((None, S5_ROW, S5_ROW), lambda b, g: (g, 0, 0)),
            pl.BlockSpec((None, S5_SLANES, S5_ROW), lambda b, g: (g, 0, 0)),
        ],
        out_specs=pl.BlockSpec((None, None, nchunk, S5_ROW), lambda b, g: (b, g, 0, 0)),
        out_shape=jax.ShapeDtypeStruct((bsz, S5_GROUPS, nchunk, S5_ROW), F32),
        compiler_params=_cparams("arbitrary", "arbitrary"),
        name="s5_output",
    )(xg, hg, m_toep, c_mat)


def _to_chunk_rows(u):
    bsz, l, _ = u.shape
    t = u.reshape(bsz, l // S5_CHUNK, S5_CHUNK, S5_GROUPS, S5_GROUP)
    return t.transpose(0, 3, 1, 2, 4).reshape(bsz, S5_GROUPS, l // S5_CHUNK, S5_ROW)


def _from_chunk_rows(y):
    bsz, _, nchunk, _ = y.shape
    t = y.reshape(bsz, S5_GROUPS, nchunk, S5_CHUNK, S5_GROUP)
    return t.transpose(0, 2, 3, 1, 4).reshape(bsz, nchunk * S5_CHUNK, S5_WIDTH)


def _s5_branch(u_c, u_l, mats, need_ctx):
    m_toep, c_mat, b_mat, a1, a2 = mats
    bsz = u_l.shape[0]
    xg_c = _to_chunk_rows(u_c)
    xg_l = _to_chunk_rows(u_l)
    nc_c = xg_c.shape[2]
    s_c = _s5_inject(xg_c, b_mat)
    s_l = _s5_inject(xg_l, b_mat)
    s_all = jnp.concatenate([s_c, s_l], axis=2)
    s_all = s_all.transpose(2, 0, 1, 3).reshape(-1, bsz * S5_GROUPS, S5_SLANES)
    a1r = jnp.tile(a1, (1, bsz, 1))
    a2r = jnp.tile(a2, (1, bsz, 1))
    hf, hb = _s5_scan(s_all, a1r, a2r)
    h_all = jnp.concatenate([hf, hb], axis=-1)
    h_all = h_all.reshape(-1, bsz, S5_GROUPS, S5_SLANES).transpose(1, 2, 0, 3)
    y_l = _from_chunk_rows(_s5_output(xg_l, h_all[:, :, nc_c:], m_toep, c_mat))
    y_c = None
    if need_ctx:
        y_c = _from_chunk_rows(_s5_output(xg_c, h_all[:, :, :nc_c], m_toep, c_mat))
    return y_c, y_l


def _gla_direction(q, k, v, lr, wg, bg, tri, causal_mask, s_ref, reverse):
    z = jnp.dot(lr.astype(BF16), wg, preferred_element_type=F32) + bg
    g = (jnp.minimum(z, 0.0) - jnp.log1p(jnp.exp(-jnp.abs(z)))) * (1.0 / GLA_NORMALIZER)
    gc = jnp.dot(tri, g, preferred_element_type=F32, precision=lax.Precision.HIGHEST)
    qt = q * jnp.exp(gc) * (GLA_DK ** -0.5)
    gct = gc.T
    kt = k.T
    outs = [None, None]
    order = (1, 0) if reverse else (0, 1)
    for ci in order:
        r0 = ci * GLA_CHUNK
        last = r0 if reverse else r0 + GLA_CHUNK - 1
        gct_c = gct[:, r0:r0 + GLA_CHUNK]
        kt_c = kt[:, r0:r0 + GLA_CHUNK]
        g_last = gct[:, last:last + 1]
        k_in = (kt_c * jnp.exp(-gct_c)).astype(BF16)
        k_st = (kt_c * jnp.exp(g_last - gct_c)).astype(BF16)
        dec = jnp.exp(g_last)
        qt_c = qt[r0:r0 + GLA_CHUNK].astype(BF16)
        v_c = v[r0:r0 + GLA_CHUNK].astype(BF16)
        o_heads = []
        for h in range(GLA_HEADS):
            ks = slice(h * GLA_DK, (h + 1) * GLA_DK)
            vs = slice(h * GLA_DV, (h + 1) * GLA_DV)
            q_h = qt_c[:, ks]
            v_h = v_c[:, vs]
            s_h = s_ref[ks, :]
            sc = jnp.dot(q_h, k_in[ks, :], preferred_element_type=F32)
            sc = jnp.where(causal_mask, sc, 0.0).astype(BF16)
            o_h = jnp.dot(sc, v_h, preferred_element_type=F32)
            o_h = o_h + jnp.dot(q_h, s_h.astype(BF16), preferred_element_type=F32)
            s_ref[ks, :] = dec[ks, :] * s_h + jnp.dot(k_st[ks, :], v_h, preferred_element_type=F32)
            o_heads.append(o_h)
        outs[ci] = jnp.concatenate(o_heads, axis=1)
    return jnp.concatenate(outs, axis=0)


def _gla_kernel(qf_ref, kf_ref, vf_ref, lrf_ref, qb_ref, kb_ref, vb_ref, lrb_ref, wg_ref, bg_ref, s0_ref,
                of_ref, ob_ref, sfin_ref, sf_scr, sb_scr):
    w = pl.program_id(1)

    @pl.when(w == 0)
    def _():
        sf_scr[...] = s0_ref[0]
        sb_scr[...] = s0_ref[1]

    row = lax.broadcasted_iota(jnp.int32, (GLA_COL, GLA_COL), 0)
    col = lax.broadcasted_iota(jnp.int32, (GLA_COL, GLA_COL), 1)
    same_chunk = (row // GLA_CHUNK) == (col // GLA_CHUNK)
    tri_f = jnp.where(same_chunk & (col <= row), 1.0, 0.0).astype(F32)
    tri_b = jnp.where(same_chunk & (col >= row), 1.0, 0.0).astype(F32)
    r64 = lax.broadcasted_iota(jnp.int32, (GLA_CHUNK, GLA_CHUNK), 0)
    c64 = lax.broadcasted_iota(jnp.int32, (GLA_CHUNK, GLA_CHUNK), 1)

    of_ref[...] = _gla_direction(qf_ref[...], kf_ref[...], vf_ref[...], lrf_ref[...], wg_ref[0], bg_ref[0],
                                 tri_f, c64 <= r64, sf_scr, reverse=False)
    ob_ref[...] = _gla_direction(qb_ref[...], kb_ref[...], vb_ref[...], lrb_ref[...], wg_ref[1], bg_ref[1],
                                 tri_b, c64 >= r64, sb_scr, reverse=True)

    @pl.when(w == pl.num_programs(1) - 1)
    def _():
        sfin_ref[0] = sf_scr[...]
        sfin_ref[1] = sb_scr[...]


def _gla(q, k, v, lr, wg, bg, s0, colmajor):
    bsz, l, _ = q.shape
    if colmajor:
        ncol = GRID_W
        assert l == GLA_COL * GRID_W
        view = lambda t: t.reshape(bsz, GLA_COL, -1)
        fwd = lambda b, w: (b, 0, w)
        bwd = lambda b, w: (b, 0, ncol - 1 - w)
    else:
        ncol = l // GLA_COL
        view = lambda t: t
        fwd = lambda b, w: (b, w, 0)
        bwd = lambda b, w: (b, ncol - 1 - w, 0)
    args = [view(t) for t in (q, k, v, lr)]
    widths = (GLA_KEY, GLA_KEY, GLA_WIDTH, LR_PAD)
    in_specs = ([pl.BlockSpec((None, GLA_COL, wd), fwd) for wd in widths]
                + [pl.BlockSpec((None, GLA_COL, wd), bwd) for wd in widths]
                + [pl.BlockSpec((2, LR_PAD, GLA_KEY), lambda b, w: (0, 0, 0)),
                   pl.BlockSpec((2, 1, GLA_KEY), lambda b, w: (0, 0, 0)),
                   pl.BlockSpec((None, 2, GLA_KEY, GLA_DV), lambda b, w: (b, 0, 0, 0))])
    o_shape = jax.ShapeDtypeStruct(args[2].shape, F32)
    o_f, o_b, s_fin = pl.pallas_call(
        _gla_kernel,
        grid=(bsz, ncol),
        in_specs=in_specs,
        out_specs=[pl.BlockSpec((None, GLA_COL, GLA_WIDTH), fwd),
                   pl.BlockSpec((None, GLA_COL, GLA_WIDTH), bwd),
                   pl.BlockSpec((None, 2, GLA_KEY, GLA_DV), lambda b, w: (b, 0, 0, 0))],
        out_shape=[o_shape, o_shape, jax.ShapeDtypeStruct((bsz, 2, GLA_KEY, GLA_DV), F32)],
        scratch_shapes=[pltpu.VMEM((GLA_KEY, GLA_DV), F32), pltpu.VMEM((GLA_KEY, GLA_DV), F32)],
        compiler_params=_cparams("arbitrary", "arbitrary"),
        name="gla",
    )(*args, *args, wg, bg, s0)
    return o_f.reshape(bsz, l, GLA_WIDTH), o_b.reshape(bsz, l, GLA_WIDTH), s_fin


def _finish_kernel(ys_ref, zs_ref, of_ref, ob_ref, zg_ref, x_ref, mod_ref, wglu_ref, bglu_ref, gn_ref, wout_ref,
                   fin_ref, o_ref, *, final):
    ys = ys_ref[...]
    s = jax.nn.gelu(ys, approximate=True)
    t = jnp.dot(s.astype(BF16), wglu_ref[...], preferred_element_type=F32) + bglu_ref[...]
    zs = zs_ref[...]
    a_s = (s * jax.nn.sigmoid(t)) * (zs * jax.nn.sigmoid(zs))
    o = of_ref[...] + ob_ref[...]
    zg = zg_ref[...]
    gate_g = zg * jax.nn.sigmoid(zg)
    y = jnp.dot(a_s.astype(BF16), wout_ref[0:S5_WIDTH, :], preferred_element_type=F32)
    for h in range(GLA_HEADS):
        vs = slice(h * GLA_DV, (h + 1) * GLA_DV)
        o_h = o[:, vs]
        o_n = o_h * lax.rsqrt(jnp.mean(o_h * o_h, axis=-1, keepdims=True) + EPS) * gn_ref[...]
        a_h = (o_n * gate_g[:, vs]).astype(BF16)
        y = y + jnp.dot(a_h, wout_ref[S5_WIDTH + h * GLA_DV:S5_WIDTH + (h + 1) * GLA_DV, :],
                        preferred_element_type=F32)
    gate = mod_ref[:, 2 * D_MODEL:3 * D_MODEL]
    xn = x_ref[...] + gate * y
    if final:
        ms = jnp.mean(xn * xn, axis=-1, keepdims=True)
        xn = xn * lax.rsqrt(ms + EPS) * fin_ref[...]
    o_ref[...] = xn


def _finish(ys, zs, o_f, o_b, zg, x2, mods, mod_row, w_glu, b_glu, gnorm, w_out, fin_gain, final):
    rows = x2.shape[0]
    tb = min(ROW_BLOCK, rows)
    rb = lambda w: pl.BlockSpec((tb, w), lambda i: (i, 0))
    full = lambda a: pl.BlockSpec(a.shape, lambda i: (0,) * a.ndim)
    return pl.pallas_call(
        functools.partial(_finish_kernel, final=final),
        grid=(rows // tb,),
        in_specs=[rb(S5_WIDTH), rb(S5_WIDTH), rb(GLA_WIDTH), rb(GLA_WIDTH), rb(GLA_WIDTH), rb(D_MODEL),
                  pl.BlockSpec((None, 1, 3 * D_MODEL), lambda i: (mod_row(i), 0, 0)),
                  full(w_glu), full(b_glu), full(gnorm), full(w_out), full(fin_gain)],
        out_specs=rb(D_MODEL),
        out_shape=jax.ShapeDtypeStruct((rows, D_MODEL), F32),
        compiler_params=_cparams("arbitrary"),
        name="finish",
    )(ys, zs, o_f, o_b, zg, x2, mods, w_glu, b_glu, gnorm, w_out, fin_gain)


def _layer(x_lat, x_ctx, mods, lw, need_ctx_out, final):
    bsz, l, _ = x_lat.shape
    lc = x_ctx.shape[1]
    blocks_per_batch = l // ROW_BLOCK
    lat_row = lambda i: i // blocks_per_batch
    ctx_row = lambda i: bsz

    pj_l = _inproj(x_lat.reshape(bsz * l, D_MODEL), mods, lw["gain"], lw["w_in"], lat_row)
    pj_c = _inproj(x_ctx.reshape(bsz * lc, D_MODEL), mods, lw["gain"], lw["w_in"], ctx_row)
    u_l, zs_l, q_l, k_l, v_l, zg_l, lr_l = pj_l
    u_c, zs_c, q_c, k_c, v_c, zg_c, lr_c = pj_c
    sh = lambda t, n: t.reshape(bsz, n, t.shape[-1])

    ys_c, ys_l = _s5_branch(sh(u_c, lc), sh(u_l, l), lw["s5"], need_ctx_out)

    s_zero = jnp.zeros((bsz, 2, GLA_KEY, GLA_DV), F32)
    ocf, ocb, s_ctx = _gla(sh(q_c, lc), sh(k_c, lc), sh(v_c, lc), sh(lr_c, lc), lw["wg"], lw["bg"], s_zero,
                           colmajor=False)
    olf, olb, _ = _gla(sh(q_l, l), sh(k_l, l), sh(v_l, l), sh(lr_l, l), lw["wg"], lw["bg"], s_ctx,
                       colmajor=True)

    fin_args = (lw["w_glu"], lw["b_glu"], lw["gnorm"], lw["w_out"], lw["fin"])
    x_lat_new = _finish(ys_l.reshape(bsz * l, S5_WIDTH), zs_l, olf.reshape(bsz * l, GLA_WIDTH),
                        olb.reshape(bsz * l, GLA_WIDTH), zg_l, x_lat.reshape(bsz * l, D_MODEL), mods, lat_row,
                        *fin_args, final=final).reshape(bsz, l, D_MODEL)
    x_ctx_new = x_ctx
    if need_ctx_out:
        x_ctx_new = _finish(ys_c.reshape(bsz * lc, S5_WIDTH), zs_c, ocf.reshape(bsz * lc, GLA_WIDTH),
                            ocb.reshape(bsz * lc, GLA_WIDTH), zg_c, x_ctx.reshape(bsz * lc, D_MODEL), mods,
                            ctx_row, *fin_args, final=False).reshape(bsz, lc, D_MODEL)
    return x_lat_new, x_ctx_new


def _layer_weights(i, norm_g, w_in, s5_lam_re, s5_lam_im, s5_log_dt, s5_b_re, s5_b_im, s5_c_re, s5_c_im, s5_d,
                   s5_w_glu, s5_b_glu, gla_w_gate, gla_b_gate, gla_norm_g, w_out, final_norm):
    lr0 = 2 * S5_WIDTH + 2 * GLA_KEY + 2 * GLA_WIDTH
    w_pad = jnp.pad(w_in[i], ((0, 0), (0, IN_PAD - w_in.shape[-1]))).astype(BF16)
    del lr0
    wg = jnp.zeros((2, LR_PAD, GLA_KEY), F32)
    for d in range(2):
        wg = wg.at[d, d * GLA_RANK:(d + 1) * GLA_RANK].set(gla_w_gate[i, d])
    return {
        "gain": norm_g[i].reshape(1, D_MODEL),
        "w_in": w_pad,
        "s5": _s5_matrices(s5_lam_re[i], s5_lam_im[i], s5_log_dt[i], s5_b_re[i], s5_b_im[i], s5_c_re[i],
                           s5_c_im[i], s5_d[i]),
        "wg": wg.astype(BF16),
        "bg": gla_b_gate[i].reshape(2, 1, GLA_KEY),
        "w_glu": s5_w_glu[i].astype(BF16),
        "b_glu": s5_b_glu[i].reshape(1, S5_WIDTH),
        "gnorm": gla_norm_g[i].reshape(1, GLA_DV),
        "w_out": w_out[i].astype(BF16),
        "fin": final_norm.reshape(1, D_MODEL),
    }


def kernel(x, c, ctx, c_ctx, norm_g, w_mod, b_mod, w_in, s5_lam_re, s5_lam_im, s5_log_dt, s5_b_re, s5_b_im,
           s5_c_re, s5_c_im, s5_d, s5_w_glu, s5_b_glu, gla_w_gate, gla_b_gate, gla_norm_g, w_out, final_norm):
    bsz = x.shape[0]
    cond8 = jnp.concatenate([c, c_ctx[None], jnp.zeros((8 - bsz - 1, D_MODEL), F32)], axis=0)
    mods_all = _modulation(cond8, w_mod, b_mod)
    x_lat, x_ctx = x, ctx
    for i in range(DEPTH):
        lw = _layer_weights(i, norm_g, w_in, s5_lam_re, s5_lam_im, s5_log_dt, s5_b_re, s5_b_im, s5_c_re,
                            s5_c_im, s5_d, s5_w_glu, s5_b_glu, gla_w_gate, gla_b_gate, gla_norm_g, w_out,
                            final_norm)
        mods = mods_all[i].reshape(8, 1, 3 * D_MODEL)
        last = i == DEPTH - 1
        x_lat, x_ctx = _layer(x_lat, x_ctx, mods, lw, need_ctx_out=not last, final=last)
    return x_lat
```

```python
import functools
import math

import jax
import jax.numpy as jnp
from jax import lax
from jax.experimental import pallas as pl
from jax.experimental.pallas import tpu as pltpu

F32 = jnp.float32
BF16 = jnp.bfloat16

D_MODEL = 1024
DEPTH = 4
GRID_W = 64
EPS = 1e-6
S5_WIDTH = 512
S5_GROUP = 16
S5_GROUPS = 32
S5_STATE = 64
S5_CHUNK = 16
S5_ROW = S5_CHUNK * S5_GROUP
S5_SLANES = 4 * S5_STATE
GLA_HEADS = 4
GLA_DV = 128
GLA_DK = 64
GLA_WIDTH = 512
GLA_KEY = 256
GLA_RANK = 16
GLA_NORMALIZER = 16.0
GLA_CHUNK = 64
GLA_COL = 128
LR_PAD = 128
IN_PAD = 2 * S5_WIDTH + 2 * GLA_KEY + 2 * GLA_WIDTH + LR_PAD

ROW_BLOCK = 512
VMEM_LIMIT = 48 * 1024 * 1024


def _cparams(*sem):
    return pltpu.CompilerParams(dimension_semantics=sem, vmem_limit_bytes=VMEM_LIMIT)


def _mod_kernel(cond_ref, w_ref, b_ref, o_ref):
    cnd = cond_ref[...]
    a = cnd * jax.nn.sigmoid(cnd)
    o_ref[...] = jnp.dot(a, w_ref[...], preferred_element_type=F32,
                         precision=lax.Precision.HIGHEST) + b_ref[...]


def _modulation(cond8, w_mod, b_mod):
    nblk = 3
    return pl.pallas_call(
        _mod_kernel,
        grid=(DEPTH, nblk),
        in_specs=[
            pl.BlockSpec((8, D_MODEL), lambda l, j: (0, 0)),
            pl.BlockSpec((None, D_MODEL, D_MODEL), lambda l, j: (l, 0, j)),
            pl.BlockSpec((None, 1, D_MODEL), lambda l, j: (l, 0, j)),
        ],
        out_specs=pl.BlockSpec((None, 8, D_MODEL), lambda l, j: (l, 0, j)),
        out_shape=jax.ShapeDtypeStruct((DEPTH, 8, 3 * D_MODEL), F32),
        compiler_params=_cparams("arbitrary", "arbitrary"),
        name="modulation",
    )(cond8, w_mod, b_mod.reshape(DEPTH, 1, 3 * D_MODEL))


_IN_SEGS = (("zs", S5_WIDTH), ("q", GLA_KEY), ("k", GLA_KEY), ("v", GLA_WIDTH), ("zg", GLA_WIDTH),
            ("lr", LR_PAD))
LANES = 128
SLOTS = LANES // S5_GROUP


def _swap_slots_with_index(vs):
    lane = lax.broadcasted_iota(jnp.int32, vs[0].shape, 1)
    for d in (4, 2, 1):
        sh = d * S5_GROUP
        low = (lane & sh) == 0
        nxt = list(vs)
        for i in range(SLOTS):
            if i & d == 0:
                a, b = vs[i], vs[i + d]
                nxt[i] = jnp.where(low, a, pltpu.roll(b, sh, 1))
                nxt[i + d] = jnp.where(low, pltpu.roll(a, LANES - sh, 1), b)
        vs = nxt
    return vs


def _inproj_kernel(x_ref, mod_ref, gain_ref, w_ref, xg_ref, *rest):
    out_refs, u_scr = rest[:-1], rest[-1]
    x = x_ref[...]
    ms = jnp.mean(x * x, axis=-1, keepdims=True)
    y = x * lax.rsqrt(ms + EPS) * gain_ref[...]
    shift = mod_ref[:, 0:D_MODEL]
    scale = mod_ref[:, D_MODEL:2 * D_MODEL]
    h = (y * (1.0 + scale) + shift).astype(BF16)
    u = jnp.dot(h, w_ref[:, 0:S5_WIDTH], preferred_element_type=F32)
    for kt in range(S5_WIDTH // LANES):
        u_scr[kt] = u[:, kt * LANES:(kt + 1) * LANES]
    off = S5_WIDTH
    for (_, width), o_ref in zip(_IN_SEGS, out_refs):
        o_ref[...] = jnp.dot(h, w_ref[:, off:off + width], preferred_element_type=F32)
        off += width
    nchunk = u_scr.shape[1] // S5_CHUNK
    for half in range(S5_CHUNK // SLOTS):
        for kt in range(S5_WIDTH // LANES):
            vs = [u_scr[kt, pl.ds(half * SLOTS + jj, nchunk, stride=S5_CHUNK), :] for jj in range(SLOTS)]
            ws = _swap_slots_with_index(vs)
            for qq in range(SLOTS):
                xg_ref[kt * SLOTS + qq, :, half * LANES:(half + 1) * LANES] = ws[qq].astype(BF16)


def _inproj(x2, mods, gain, w_pad, mod_row, bsz):
    rows = x2.shape[0]
    per_seq = rows // bsz
    tb = min(ROW_BLOCK, per_seq)
    bps = per_seq // tb
    return pl.pallas_call(
        _inproj_kernel,
        grid=(rows // tb,),
        in_specs=[
            pl.BlockSpec((tb, D_MODEL), lambda i: (i, 0)),
            pl.BlockSpec((None, 1, 3 * D_MODEL), lambda i: (mod_row(i), 0, 0)),
            pl.BlockSpec((1, D_MODEL), lambda i: (0, 0)),
            pl.BlockSpec((D_MODEL, IN_PAD), lambda i: (0, 0)),
        ],
        out_specs=[pl.BlockSpec((None, S5_GROUPS, tb // S5_CHUNK, S5_ROW), lambda i: (i // bps, 0, i % bps, 0))]
        + [pl.BlockSpec((tb, w), lambda i: (i, 0)) for _, w in _IN_SEGS],
        out_shape=[jax.ShapeDtypeStruct((bsz, S5_GROUPS, per_seq // S5_CHUNK, S5_ROW), BF16)]
        + [jax.ShapeDtypeStruct((rows, w), F32) for _, w in _IN_SEGS],
        scratch_shapes=[pltpu.VMEM((S5_WIDTH // LANES, tb, LANES), F32)],
        compiler_params=_cparams("arbitrary"),
        name="inproj",
    )(x2, mods, gain, w_pad)


def _s5_matrices(lam_re, lam_im, log_dt, b_re, b_im, c_re, c_im, d_skip):
    hp = lax.Precision.HIGHEST
    T = S5_CHUNK
    lam = lax.complex(lam_re.astype(F32), lam_im.astype(F32))
    dt = jnp.exp(log_dt.astype(F32))[..., None]
    ldt = lam * dt
    lam_bar = jnp.exp(ldt)
    b_c = lax.complex(b_re.astype(F32), b_im.astype(F32))
    b_bar = ((lam_bar - 1.0) / lam)[..., None] * b_c[None]
    c_c = lax.complex(c_re.astype(F32), c_im.astype(F32))
    steps = jnp.arange(T + 1, dtype=F32)
    pw = jnp.exp(ldt[:, None] * steps[None, :, None, None].astype(jnp.complex64))

    kern = jnp.einsum('gpn,dtgn,dgnq->dgtpq', c_c, pw[:, :T], b_bar, precision=hp).real
    ii = jnp.arange(T)[:, None]
    jj = jnp.arange(T)[None, :]
    lag_f = jnp.clip(jj - ii, 0, T - 1)
    lag_b = jnp.clip(ii - jj, 0, T - 1)
    kf = jnp.where((jj >= ii)[None, :, :, None, None], kern[0][:, lag_f], 0.0)
    kb = jnp.where((ii >= jj)[None, :, :, None, None], kern[1][:, lag_b], 0.0)
    eye_t = jnp.eye(T, dtype=F32)[None, :, :, None, None]
    eye_p = jnp.eye(S5_GROUP, dtype=F32)[None, None, None]
    d_g = d_skip.astype(F32).reshape(S5_GROUPS, 1, 1, S5_GROUP, 1)
    m5 = kf + kb + eye_t * eye_p * d_g
    m_toep = m5.transpose(0, 1, 4, 2, 3).reshape(S5_GROUPS, S5_ROW, S5_ROW)

    pf = pw[0][1:T + 1]
    pb = pw[1][::-1][:T]

    def readout(pj):
        wc = c_c[:, None] * pj.transpose(1, 0, 2)[:, :, None, :]
        wc = wc.transpose(0, 3, 1, 2).reshape(S5_GROUPS, S5_STATE, S5_ROW)
        return jnp.concatenate([wc.real, -wc.imag], axis=1)

    c_mat = jnp.concatenate([readout(pf), readout(pb)], axis=1)

    qf = pw[0][:T][::-1]
    qb = pw[1][:T]

    def inject(qi, bb):
        wb = qi.transpose(1, 0, 2)[..., None] * bb[:, None]
        wb = wb.transpose(0, 1, 3, 2).reshape(S5_GROUPS, S5_ROW, S5_STATE)
        return jnp.concatenate([wb.real, wb.imag], axis=2)

    b_mat = jnp.concatenate([inject(qf, b_bar[0]), inject(qb, b_bar[1])], axis=2)

    a_chunk = pw[:, T]
    a1 = jnp.concatenate([a_chunk.real, a_chunk.real], axis=-1)
    a2 = jnp.concatenate([-a_chunk.imag, a_chunk.imag], axis=-1)
    return m_toep.astype(BF16), c_mat.astype(BF16), b_mat.astype(BF16), a1, a2


def _s5_inject_kernel(x_ref, b_ref, s_ref):
    s_ref[...] = jnp.dot(x_ref[...], b_ref[...], preferred_element_type=F32)


def _s5_inject(xg, b_mat):
    bsz, _, nchunk, _ = xg.shape
    return pl.pallas_call(
        _s5_inject_kernel,
        grid=(bsz, S5_GROUPS),
        in_specs=[
            pl.BlockSpec((None, None, nchunk, S5_ROW), lambda b, g: (b, g, 0, 0)),
            pl.BlockSpec((None, S5_ROW, S5_SLANES), lambda b, g: (g, 0, 0)),
        ],
        out_specs=pl.BlockSpec((None, None, nchunk, S5_SLANES), lambda b, g: (b, g, 0, 0)),
        out_shape=jax.ShapeDtypeStruct((bsz, S5_GROUPS, nchunk, S5_SLANES), F32),
        compiler_params=_cparams("arbitrary", "arbitrary"),
        name="s5_inject",
    )(xg, b_mat)


S5_SCAN_BLOCK = 32
S5_HALF = 2 * S5_STATE


def _s5_scan_kernel(sf_ref, sb_ref, a1_ref, a2_ref, h0_ref, hf_ref, hb_ref, hfin_ref, hf_scr, hb_scr, *, cpb):
    @pl.when(pl.program_id(0) == 0)
    def _():
        hf_scr[...] = h0_ref[0]
        hb_scr[...] = h0_ref[1]

    a1f = a1_ref[0]
    a2f = a2_ref[0]
    a1b = a1_ref[1]
    a2b = a2_ref[1]
    hf = hf_scr[...]
    hb = hb_scr[...]
    for c in range(cpb):
        hf_ref[:, c * S5_HALF:(c + 1) * S5_HALF] = hf
        hf = a1f * hf + a2f * pltpu.roll(hf, S5_STATE, 1) + sf_ref[:, c * S5_SLANES:c * S5_SLANES + S5_HALF]
        cb = cpb - 1 - c
        hb_ref[:, cb * S5_HALF:(cb + 1) * S5_HALF] = hb
        hb = (a1b * hb + a2b * pltpu.roll(hb, S5_STATE, 1)
              + sb_ref[:, cb * S5_SLANES + S5_HALF:(cb + 1) * S5_SLANES])
    hf_scr[...] = hf
    hb_scr[...] = hb

    @pl.when(pl.program_id(0) == pl.num_programs(0) - 1)
    def _():
        hfin_ref[0] = hf
        hfin_ref[1] = hb


def _s5_scan(s_loc, a1, a2, h0):
    bsz, _, nchunk, _ = s_loc.shape
    rows = bsz * S5_GROUPS
    cpb = min(S5_SCAN_BLOCK, nchunk)
    nblk = nchunk // cpb
    s2 = s_loc.reshape(rows, nchunk * S5_SLANES)
    hf, hb, hfin = pl.pallas_call(
        functools.partial(_s5_scan_kernel, cpb=cpb),
        grid=(nblk,),
        in_specs=[
            pl.BlockSpec((rows, cpb * S5_SLANES), lambda i: (0, i)),
            pl.BlockSpec((rows, cpb * S5_SLANES), lambda i: (0, nblk - 1 - i)),
            pl.BlockSpec((2, rows, S5_HALF), lambda i: (0, 0, 0)),
            pl.BlockSpec((2, rows, S5_HALF), lambda i: (0, 0, 0)),
            pl.BlockSpec((2, rows, S5_HALF), lambda i: (0, 0, 0)),
        ],
        out_specs=[
            pl.BlockSpec((rows, cpb * S5_HALF), lambda i: (0, i)),
            pl.BlockSpec((rows, cpb * S5_HALF), lambda i: (0, nblk - 1 - i)),
            pl.BlockSpec((2, rows, S5_HALF), lambda i: (0, 0, 0)),
        ],
        out_shape=[jax.ShapeDtypeStruct((rows, nchunk * S5_HALF), F32)] * 2
        + [jax.ShapeDtypeStruct((2, rows, S5_HALF), F32)],
        scratch_shapes=[pltpu.VMEM((rows, S5_HALF), F32), pltpu.VMEM((rows, S5_HALF), F32)],
        compiler_params=_cparams("arbitrary"),
        name="s5_scan",
    )(s2, s2, a1, a2, h0)
    shape4 = (bsz, S5_GROUPS, nchunk, S5_HALF)
    return hf.reshape(shape4), hb.reshape(shape4), hfin


def _s5_output_kernel(x_ref, hf_ref, hb_ref, m_ref, c_ref, y_ref):
    y = jnp.dot(x_ref[...], m_ref[...], preferred_element_type=F32)
    y = y + jnp.dot(hf_ref[...].astype(BF16), c_ref[0:S5_HALF, :], preferred_element_type=F32)
    y_ref[...] = y + jnp.dot(hb_ref[...].astype(BF16), c_ref[S5_HALF:2 * S5_HALF, :], preferred_element_type=F32)


def _s5_output(xg, hf, hb, m_toep, c_mat):
    bsz, _, nchunk, _ = xg.shape
    return pl.pallas_call(
        _s5_output_kernel,
        grid=(bsz, S5_GROUPS),
        in_specs=[
            pl.BlockSpec((None, None, nchunk, S5_ROW), lambda b, g: (b, g, 0, 0)),
            pl.BlockSpec((None, None, nchunk, S5_HALF), lambda b, g: (b, g, 0, 0)),
            pl.BlockSpec((None, None, nchunk, S5_HALF), lambda b, g: (b, g, 0, 0)),
            pl.BlockSpec((None, S5_ROW, S5_ROW), lambda b, g: (g, 0, 0)),
            pl.BlockSpec((None, S5_SLANES, S5_ROW), lambda b, g: (g, 0, 0)),
        ],
        out_specs=pl.BlockSpec((None, None, nchunk, S5_ROW), lambda b, g: (b, g, 0, 0)),
        out_shape=jax.ShapeDtypeStruct((bsz, S5_GROUPS, nchunk, S5_ROW), F32),
        compiler_params=_cparams("arbitrary", "arbitrary"),
        name="s5_output",
    )(xg, hf, hb, m_toep, c_mat)


def _s5_branch(xg_c, xg_l, mats, need_ctx):
    m_toep, c_mat, b_mat, a1, a2 = mats
    bsz = xg_l.shape[0]
    a1r = jnp.tile(a1, (1, bsz, 1))
    a2r = jnp.tile(a2, (1, bsz, 1))
    s_c = _s5_inject(xg_c, b_mat)
    s_l = _s5_inject(xg_l, b_mat)
    hf_c, hb_c, h_ctx = _s5_scan(s_c, a1r, a2r, jnp.zeros_like(a1r))
    hf_l, hb_l, _ = _s5_scan(s_l, a1r, a2r, h_ctx)
    y_l = _s5_output(xg_l, hf_l, hb_l, m_toep, c_mat)
    y_c = _s5_output(xg_c, hf_c, hb_c, m_toep, c_mat) if need_ctx else None
    return y_c, y_l


def _gla_direction(q, k, v, lr, wg, bg, tri, causal_mask, s_ref, reverse):
    z = jnp.dot(lr.astype(BF16), wg, preferred_element_type=F32) + bg
    g = (jnp.minimum(z, 0.0) - jnp.log1p(jnp.exp(-jnp.abs(z)))) * (1.0 / GLA_NORMALIZER)
    gc = jnp.dot(tri, g, preferred_element_type=F32, precision=lax.Precision.HIGHEST)
    qt = q * jnp.exp(gc) * (GLA_DK ** -0.5)
    gct = gc.T
    kt = k.T
    outs = [None, None]
    order = (1, 0) if reverse else (0, 1)
    for ci in order:
        r0 = ci * GLA_CHUNK
        last = r0 if reverse else r0 + GLA_CHUNK - 1
        gct_c = gct[:, r0:r0 + GLA_CHUNK]
        kt_c = kt[:, r0:r0 + GLA_CHUNK]
        g_last = gct[:, last:last + 1]
        k_in = (kt_c * jnp.exp(-gct_c)).astype(BF16)
        k_st = (kt_c * jnp.exp(g_last - gct_c)).astype(BF16)
        dec = jnp.exp(g_last)
        qt_c = qt[r0:r0 + GLA_CHUNK].astype(BF16)
        v_c = v[r0:r0 + GLA_CHUNK].astype(BF16)
        o_heads = []
        for h in range(GLA_HEADS):
            ks = slice(h * GLA_DK, (h + 1) * GLA_DK)
            vs = slice(h * GLA_DV, (h + 1) * GLA_DV)
            q_h = qt_c[:, ks]
            v_h = v_c[:, vs]
            s_h = s_ref[ks, :]
            sc = jnp.dot(q_h, k_in[ks, :], preferred_element_type=F32)
            sc = jnp.where(causal_mask, sc, 0.0).astype(BF16)
            o_h = jnp.dot(sc, v_h, preferred_element_type=F32)
            o_h = o_h + jnp.dot(q_h, s_h.astype(BF16), preferred_element_type=F32)
            s_ref[ks, :] = dec[ks, :] * s_h + jnp.dot(k_st[ks, :], v_h, preferred_element_type=F32)
            o_heads.append(o_h)
        outs[ci] = jnp.concatenate(o_heads, axis=1)
    return jnp.concatenate(outs, axis=0)


def _gla_kernel(qf_ref, kf_ref, vf_ref, lrf_ref, qb_ref, kb_ref, vb_ref, lrb_ref, wg_ref, bg_ref, s0_ref,
                of_ref, ob_ref, sfin_ref, sf_scr, sb_scr):
    w = pl.program_id(1)

    @pl.when(w == 0)
    def _():
        sf_scr[...] = s0_ref[0]
        sb_scr[...] = s0_ref[1]

    row = lax.broadcasted_iota(jnp.int32, (GLA_COL, GLA_COL), 0)
    col = lax.broadcasted_iota(jnp.int32, (GLA_COL, GLA_COL), 1)
    same_chunk = (row // GLA_CHUNK) == (col // GLA_CHUNK)
    tri_f = jnp.where(same_chunk & (col <= row), 1.0, 0.0).astype(F32)
    tri_b = jnp.where(same_chunk & (col >= row), 1.0, 0.0).astype(F32)
    r64 = lax.broadcasted_iota(jnp.int32, (GLA_CHUNK, GLA_CHUNK), 0)
    c64 = lax.broadcasted_iota(jnp.int32, (GLA_CHUNK, GLA_CHUNK), 1)

    of_ref[...] = _gla_direction(qf_ref[...], kf_ref[...], vf_ref[...], lrf_ref[...], wg_ref[0], bg_ref[0],
                                 tri_f, c64 <= r64, sf_scr, reverse=False)
    ob_ref[...] = _gla_direction(qb_ref[...], kb_ref[...], vb_ref[...], lrb_ref[...], wg_ref[1], bg_ref[1],
                                 tri_b, c64 >= r64, sb_scr, reverse=True)

    @pl.when(w == pl.num_programs(1) - 1)
    def _():
        sfin_ref[0] = sf_scr[...]
        sfin_ref[1] = sb_scr[...]


def _gla(q, k, v, lr, wg, bg, s0, colmajor):
    bsz, l, _ = q.shape
    if colmajor:
        ncol = GRID_W
        assert l == GLA_COL * GRID_W
        view = lambda t: t.reshape(bsz, GLA_COL, -1)
        fwd = lambda b, w: (b, 0, w)
        bwd = lambda b, w: (b, 0, ncol - 1 - w)
    else:
        ncol = l // GLA_COL
        view = lambda t: t
        fwd = lambda b, w: (b, w, 0)
        bwd = lambda b, w: (b, ncol - 1 - w, 0)
    args = [view(t) for t in (q, k, v, lr)]
    widths = (GLA_KEY, GLA_KEY, GLA_WIDTH, LR_PAD)
    in_specs = ([pl.BlockSpec((None, GLA_COL, wd), fwd) for wd in widths]
                + [pl.BlockSpec((None, GLA_COL, wd), bwd) for wd in widths]
                + [pl.BlockSpec((2, LR_PAD, GLA_KEY), lambda b, w: (0, 0, 0)),
                   pl.BlockSpec((2, 1, GLA_KEY), lambda b, w: (0, 0, 0)),
                   pl.BlockSpec((None, 2, GLA_KEY, GLA_DV), lambda b, w: (b, 0, 0, 0))])
    o_shape = jax.ShapeDtypeStruct(args[2].shape, F32)
    o_f, o_b, s_fin = pl.pallas_call(
        _gla_kernel,
        grid=(bsz, ncol),
        in_specs=in_specs,
        out_specs=[pl.BlockSpec((None, GLA_COL, GLA_WIDTH), fwd),
                   pl.BlockSpec((None, GLA_COL, GLA_WIDTH), bwd),
                   pl.BlockSpec((None, 2, GLA_KEY, GLA_DV), lambda b, w: (b, 0, 0, 0))],
        out_shape=[o_shape, o_shape, jax.ShapeDtypeStruct((bsz, 2, GLA_KEY, GLA_DV), F32)],
        scratch_shapes=[pltpu.VMEM((GLA_KEY, GLA_DV), F32), pltpu.VMEM((GLA_KEY, GLA_DV), F32)],
        compiler_params=_cparams("arbitrary", "arbitrary"),
        name="gla",
    )(*args, *args, wg, bg, s0)
    return o_f.reshape(bsz, l, GLA_WIDTH), o_b.reshape(bsz, l, GLA_WIDTH), s_fin


def _finish_kernel(ys_ref, zs_ref, of_ref, ob_ref, zg_ref, x_ref, mod_ref, wglu_ref, bglu_ref, gn_ref, wout_ref,
                   fin_ref, o_ref, ys_scr, *, final):
    nchunk = ys_scr.shape[1] // S5_CHUNK
    for half in range(S5_CHUNK // SLOTS):
        for kt in range(S5_WIDTH // LANES):
            ws = [ys_ref[kt * SLOTS + qq, :, half * LANES:(half + 1) * LANES] for qq in range(SLOTS)]
            vs = _swap_slots_with_index(ws)
            for jj in range(SLOTS):
                ys_scr[kt, pl.ds(half * SLOTS + jj, nchunk, stride=S5_CHUNK), :] = vs[jj]
    ys = jnp.concatenate([ys_scr[kt] for kt in range(S5_WIDTH // LANES)], axis=1)
    s = jax.nn.gelu(ys, approximate=True)
    t = jnp.dot(s.astype(BF16), wglu_ref[...], preferred_element_type=F32) + bglu_ref[...]
    zs = zs_ref[...]
    a_s = (s * jax.nn.sigmoid(t)) * (zs * jax.nn.sigmoid(zs))
    o = of_ref[...] + ob_ref[...]
    zg = zg_ref[...]
    gate_g = zg * jax.nn.sigmoid(zg)
    y = jnp.dot(a_s.astype(BF16), wout_ref[0:S5_WIDTH, :], preferred_element_type=F32)
    for h in range(GLA_HEADS):
        vs = slice(h * GLA_DV, (h + 1) * GLA_DV)
        o_h = o[:, vs]
        o_n = o_h * lax.rsqrt(jnp.mean(o_h * o_h, axis=-1, keepdims=True) + EPS) * gn_ref[...]
        a_h = (o_n * gate_g[:, vs]).astype(BF16)
        y = y + jnp.dot(a_h, wout_ref[S5_WIDTH + h * GLA_DV:S5_WIDTH + (h + 1) * GLA_DV, :],
                        preferred_element_type=F32)
    gate = mod_ref[:, 2 * D_MODEL:3 * D_MODEL]
    xn = x_ref[...] + gate * y
    if final:
        ms = jnp.mean(xn * xn, axis=-1, keepdims=True)
        xn = xn * lax.rsqrt(ms + EPS) * fin_ref[...]
    o_ref[...] = xn


def _finish(ys, zs, o_f, o_b, zg, x2, mods, mod_row, w_glu, b_glu, gnorm, w_out, fin_gain, final):
    rows = x2.shape[0]
    per_seq = rows // ys.shape[0]
    tb = min(ROW_BLOCK, per_seq)
    bps = per_seq // tb
    rb = lambda w: pl.BlockSpec((tb, w), lambda i: (i, 0))
    full = lambda a: pl.BlockSpec(a.shape, lambda i: (0,) * a.ndim)
    return pl.pallas_call(
        functools.partial(_finish_kernel, final=final),
        grid=(rows // tb,),
        scratch_shapes=[pltpu.VMEM((S5_WIDTH // LANES, tb, LANES), F32)],
        in_specs=[pl.BlockSpec((None, S5_GROUPS, tb // S5_CHUNK, S5_ROW), lambda i: (i // bps, 0, i % bps, 0)),
                  rb(S5_WIDTH), rb(GLA_WIDTH), rb(GLA_WIDTH), rb(GLA_WIDTH), rb(D_MODEL),
                  pl.BlockSpec((None, 1, 3 * D_MODEL), lambda i: (mod_row(i), 0, 0)),
                  full(w_glu), full(b_glu), full(gnorm), full(w_out), full(fin_gain)],
        out_specs=rb(D_MODEL),
        out_shape=jax.ShapeDtypeStruct((rows, D_MODEL), F32),
        compiler_params=_cparams("arbitrary"),
        name="finish",
    )(ys, zs, o_f, o_b, zg, x2, mods, w_glu, b_glu, gnorm, w_out, fin_gain)


def _layer(x_lat, x_ctx, mods, lw, need_ctx_out, final):
    bsz, l, _ = x_lat.shape
    lc = x_ctx.shape[1]
    blocks_per_batch = l // ROW_BLOCK
    lat_row = lambda i: i // blocks_per_batch
    ctx_row = lambda i: bsz

    pj_l = _inproj(x_lat.reshape(bsz * l, D_MODEL), mods, lw["gain"], lw["w_in"], lat_row, bsz)
    pj_c = _inproj(x_ctx.reshape(bsz * lc, D_MODEL), mods, lw["gain"], lw["w_in"], ctx_row, bsz)
    xg_l, zs_l, q_l, k_l, v_l, zg_l, lr_l = pj_l
    xg_c, zs_c, q_c, k_c, v_c, zg_c, lr_c = pj_c
    sh = lambda t, n: t.reshape(bsz, n, t.shape[-1])

    ys_c, ys_l = _s5_branch(xg_c, xg_l, lw["s5"], need_ctx_out)

    s_zero = jnp.zeros((bsz, 2, GLA_KEY, GLA_DV), F32)
    ocf, ocb, s_ctx = _gla(sh(q_c, lc), sh(k_c, lc), sh(v_c, lc), sh(lr_c, lc), lw["wg"], lw["bg"], s_zero,
                           colmajor=False)
    olf, olb, _ = _gla(sh(q_l, l), sh(k_l, l), sh(v_l, l), sh(lr_l, l), lw["wg"], lw["bg"], s_ctx,
                       colmajor=True)

    fin_args = (lw["w_glu"], lw["b_glu"], lw["gnorm"], lw["w_out"], lw["fin"])
    x_lat_new = _finish(ys_l, zs_l, olf.reshape(bsz * l, GLA_WIDTH),
                        olb.reshape(bsz * l, GLA_WIDTH), zg_l, x_lat.reshape(bsz * l, D_MODEL), mods, lat_row,
                        *fin_args, final=final).reshape(bsz, l, D_MODEL)
    x_ctx_new = x_ctx
    if need_ctx_out:
        x_ctx_new = _finish(ys_c, zs_c, ocf.reshape(bsz * lc, GLA_WIDTH),
                            ocb.reshape(bsz * lc, GLA_WIDTH), zg_c, x_ctx.reshape(bsz * lc, D_MODEL), mods,
                            ctx_row, *fin_args, final=False).reshape(bsz, lc, D_MODEL)
    return x_lat_new, x_ctx_new


def _layer_weights(i, norm_g, w_in, s5_lam_re, s5_lam_im, s5_log_dt, s5_b_re, s5_b_im, s5_c_re, s5_c_im, s5_d,
                   s5_w_glu, s5_b_glu, gla_w_gate, gla_b_gate, gla_norm_g, w_out, final_norm):
    w_pad = jnp.pad(w_in[i], ((0, 0), (0, IN_PAD - w_in.shape[-1]))).astype(BF16)
    wg = jnp.zeros((2, LR_PAD, GLA_KEY), F32)
    for d in range(2):
        wg = wg.at[d, d * GLA_RANK:(d + 1) * GLA_RANK].set(gla_w_gate[i, d])
    return {
        "gain": norm_g[i].reshape(1, D_MODEL),
        "w_in": w_pad,
        "s5": _s5_matrices(s5_lam_re[i], s5_lam_im[i], s5_log_dt[i], s5_b_re[i], s5_b_im[i], s5_c_re[i],
                           s5_c_im[i], s5_d[i]),
        "wg": wg.astype(BF16),
        "bg": gla_b_gate[i].reshape(2, 1, GLA_KEY),
        "w_glu": s5_w_glu[i].astype(BF16),
        "b_glu": s5_b_glu[i].reshape(1, S5_WIDTH),
        "gnorm": gla_norm_g[i].reshape(1, GLA_DV),
        "w_out": w_out[i].astype(BF16),
        "fin": final_norm.reshape(1, D_MODEL),
    }


def kernel(x, c, ctx, c_ctx, norm_g, w_mod, b_mod, w_in, s5_lam_re, s5_lam_im, s5_log_dt, s5_b_re, s5_b_im,
           s5_c_re, s5_c_im, s5_d, s5_w_glu, s5_b_glu, gla_w_gate, gla_b_gate, gla_norm_g, w_out, final_norm):
    bsz = x.shape[0]
    cond8 = jnp.concatenate([c, c_ctx[None], jnp.zeros((8 - bsz - 1, D_MODEL), F32)], axis=0)
    mods_all = _modulation(cond8, w_mod, b_mod)
    x_lat, x_ctx = x, ctx
    for i in range(DEPTH):
        lw = _layer_weights(i, norm_g, w_in, s5_lam_re, s5_lam_im, s5_log_dt, s5_b_re, s5_b_im, s5_c_re,
                            s5_c_im, s5_d, s5_w_glu, s5_b_glu, gla_w_gate, gla_b_gate, gla_norm_g, w_out,
                            final_norm)
        mods = mods_all[i].reshape(8, 1, 3 * D_MODEL)
        last = i == DEPTH - 1
        x_lat, x_ctx = _layer(x_lat, x_ctx, mods, lw, need_ctx_out=not last, final=last)
    return x_lat
```

```python
import functools

import jax
import jax.numpy as jnp
from jax import lax
from jax.experimental import pallas as pl
from jax.experimental.pallas import tpu as pltpu

F32 = jnp.float32
BF16 = jnp.bfloat16

D_MODEL = 1024
DEPTH = 4
GRID_W = 64
EPS = 1e-6
LANES = 128
SUBLANES = 8
S5_WIDTH = 512
S5_GROUP = 16
S5_GROUPS = 32
S5_STATE = 64
S5_CHUNK = 16
S5_ROW = S5_CHUNK * S5_GROUP
S5_HALF = 2 * S5_STATE
S5_SLANES = 2 * S5_HALF
S5_SCAN_BLOCK = 32
S5_GROUP_BLOCK = 8
SLOTS = LANES // S5_GROUP
GLA_HEADS = 4
GLA_DV = 128
GLA_DK = 64
GLA_WIDTH = 512
GLA_KEY = 256
GLA_RANK = 16
GLA_NORMALIZER = 16.0
GLA_CHUNK = 64
GLA_COL = 128
LR_PAD = 128
IN_PAD = 2 * S5_WIDTH + 2 * GLA_KEY + 2 * GLA_WIDTH + LR_PAD

ROW_BLOCK = 512
GRID_ROWS_PER_BLOCK = ROW_BLOCK // GRID_W
CHUNK_PITCH = S5_CHUNK + SUBLANES
GRID_ROW_PITCH = GRID_W + SUBLANES
VMEM_LIMIT = 48 * 1024 * 1024


def _cparams(*sem):
    return pltpu.CompilerParams(dimension_semantics=sem, vmem_limit_bytes=VMEM_LIMIT)


def _mod_kernel(cond_ref, w_ref, b_ref, o_ref):
    cnd = cond_ref[...]
    a = cnd * jax.nn.sigmoid(cnd)
    o_ref[...] = jnp.dot(a, w_ref[...], preferred_element_type=F32,
                         precision=lax.Precision.HIGHEST) + b_ref[...]


def _modulation(cond8, w_mod, b_mod):
    nblk = 3
    return pl.pallas_call(
        _mod_kernel,
        grid=(DEPTH, nblk),
        in_specs=[
            pl.BlockSpec((8, D_MODEL), lambda l, j: (0, 0)),
            pl.BlockSpec((None, D_MODEL, D_MODEL), lambda l, j: (l, 0, j)),
            pl.BlockSpec((None, 1, D_MODEL), lambda l, j: (l, 0, j)),
        ],
        out_specs=pl.BlockSpec((None, 8, D_MODEL), lambda l, j: (l, 0, j)),
        out_shape=jax.ShapeDtypeStruct((DEPTH, 8, 3 * D_MODEL), F32),
        compiler_params=_cparams("arbitrary", "arbitrary"),
        name="modulation",
    )(cond8, w_mod, b_mod.reshape(DEPTH, 1, 3 * D_MODEL))


_IN_SEGS = (("zs", S5_WIDTH, False), ("q", GLA_KEY, True), ("k", GLA_KEY, True), ("v", GLA_WIDTH, True),
            ("zg", GLA_WIDTH, False), ("lr", LR_PAD, True))


def _swap_slots_with_index(vs):
    lane = lax.broadcasted_iota(jnp.int32, vs[0].shape, 1)
    for d in (4, 2, 1):
        sh = d * S5_GROUP
        low = (lane & sh) == 0
        nxt = list(vs)
        for i in range(SLOTS):
            if i & d == 0:
                a, b = vs[i], vs[i + d]
                nxt[i] = jnp.where(low, a, pltpu.roll(b, sh, 1))
                nxt[i + d] = jnp.where(low, pltpu.roll(a, LANES - sh, 1), b)
        vs = nxt
    return vs


def _inproj_kernel(x_ref, mod_ref, gain_ref, w_ref, xg_ref, *rest, colmajor):
    out_refs, u_scr, c_scr = rest[:-2], rest[-2], rest[-1]
    x = x_ref[...]
    ms = jnp.mean(x * x, axis=-1, keepdims=True)
    y = x * lax.rsqrt(ms + EPS) * gain_ref[...]
    shift = mod_ref[:, 0:D_MODEL]
    scale = mod_ref[:, D_MODEL:2 * D_MODEL]
    h = (y * (1.0 + scale) + shift).astype(BF16)
    u = jnp.dot(h, w_ref[:, 0:S5_WIDTH], preferred_element_type=F32)
    nchunk = u.shape[0] // S5_CHUNK
    for kt in range(S5_WIDTH // LANES):
        for c in range(nchunk):
            u_scr[kt, c * CHUNK_PITCH:c * CHUNK_PITCH + S5_CHUNK, :] = u[c * S5_CHUNK:(c + 1) * S5_CHUNK,
                                                                         kt * LANES:(kt + 1) * LANES]
    off = S5_WIDTH
    for (_, width, cm), o_ref in zip(_IN_SEGS, out_refs):
        val = jnp.dot(h, w_ref[:, off:off + width], preferred_element_type=F32)
        off += width
        if colmajor and cm:
            for kt in range(width // LANES):
                for r in range(GRID_ROWS_PER_BLOCK):
                    c_scr[kt, r * GRID_ROW_PITCH:r * GRID_ROW_PITCH + GRID_W, :] = val[r * GRID_W:(r + 1) * GRID_W,
                                                                                       kt * LANES:(kt + 1) * LANES]
            for w in range(GRID_W):
                for kt in range(width // LANES):
                    o_ref[w, :, kt * LANES:(kt + 1) * LANES] = c_scr[kt, pl.ds(w, GRID_ROWS_PER_BLOCK,
                                                                              stride=GRID_ROW_PITCH), :]
        else:
            o_ref[...] = val
    for half in range(S5_CHUNK // SLOTS):
        for kt in range(S5_WIDTH // LANES):
            vs = [u_scr[kt, pl.ds(half * SLOTS + jj, nchunk, stride=CHUNK_PITCH), :] for jj in range(SLOTS)]
            ws = _swap_slots_with_index(vs)
            for qq in range(SLOTS):
                xg_ref[kt * SLOTS + qq, :, half * LANES:(half + 1) * LANES] = ws[qq].astype(BF16)


def _inproj(x2, mods, gain, w_pad, mod_row, bsz, colmajor):
    rows = x2.shape[0]
    per_seq = rows // bsz
    tb = min(ROW_BLOCK, per_seq)
    bps = per_seq // tb
    out_specs = [pl.BlockSpec((None, S5_GROUPS, tb // S5_CHUNK, S5_ROW), lambda i: (i // bps, 0, i % bps, 0))]
    out_shape = [jax.ShapeDtypeStruct((bsz, S5_GROUPS, per_seq // S5_CHUNK, S5_ROW), BF16)]
    for _, w, cm in _IN_SEGS:
        if colmajor and cm:
            assert tb == ROW_BLOCK
            out_specs.append(pl.BlockSpec((None, GRID_W, GRID_ROWS_PER_BLOCK, w),
                                          lambda i: (i // bps, 0, i % bps, 0)))
            out_shape.append(jax.ShapeDtypeStruct((bsz, GRID_W, per_seq // GRID_W, w), F32))
        else:
            out_specs.append(pl.BlockSpec((tb, w), lambda i: (i, 0)))
            out_shape.append(jax.ShapeDtypeStruct((rows, w), F32))
    return pl.pallas_call(
        functools.partial(_inproj_kernel, colmajor=colmajor),
        grid=(rows // tb,),
        in_specs=[
            pl.BlockSpec((tb, D_MODEL), lambda i: (i, 0)),
            pl.BlockSpec((None, 1, 3 * D_MODEL), lambda i: (mod_row(i), 0, 0)),
            pl.BlockSpec((1, D_MODEL), lambda i: (0, 0)),
            pl.BlockSpec((D_MODEL, IN_PAD), lambda i: (0, 0)),
        ],
        out_specs=out_specs,
        out_shape=out_shape,
        scratch_shapes=[pltpu.VMEM((S5_WIDTH // LANES, tb // S5_CHUNK * CHUNK_PITCH, LANES), F32),
                        pltpu.VMEM((GLA_WIDTH // LANES, GRID_ROWS_PER_BLOCK * GRID_ROW_PITCH, LANES), F32)],
        compiler_params=_cparams("arbitrary"),
        name="inproj",
    )(x2, mods, gain, w_pad)


def _toeplitz_upper(kr):
    t = kr.shape[-1]
    r = jnp.concatenate([kr, jnp.zeros_like(kr)], axis=-1)
    flat = jnp.tile(r, (1,) * (kr.ndim - 1) + (t,))[..., :t * (2 * t - 1)]
    return flat.reshape(kr.shape[:-1] + (t, 2 * t - 1))[..., :t]


def _s5_matrices(lam_re, lam_im, log_dt, b_re, b_im, c_re, c_im, d_skip):
    hp = lax.Precision.HIGHEST
    T = S5_CHUNK
    lam = lax.complex(lam_re.astype(F32), lam_im.astype(F32))
    dt = jnp.exp(log_dt.astype(F32))[..., None]
    ldt = lam * dt
    lam_bar = jnp.exp(ldt)
    b_c = lax.complex(b_re.astype(F32), b_im.astype(F32))
    b_bar = ((lam_bar - 1.0) / lam)[..., None] * b_c[None]
    c_c = lax.complex(c_re.astype(F32), c_im.astype(F32))
    steps = jnp.arange(T + 1, dtype=F32)
    pw = jnp.exp(ldt[:, None] * steps[None, :, None, None].astype(jnp.complex64))

    kern = jnp.einsum('gpn,dtgn,dgnq->dgpqt', c_c, pw[:, :T], b_bar, precision=hp).real
    a_f = _toeplitz_upper(kern[0])
    a_b = jnp.swapaxes(_toeplitz_upper(kern[1]), -1, -2)
    eye_t = jnp.eye(T, dtype=F32)[None, None, None]
    eye_p = jnp.eye(S5_GROUP, dtype=F32)[None, :, :, None, None]
    d_g = d_skip.astype(F32).reshape(S5_GROUPS, S5_GROUP, 1, 1, 1)
    m5 = a_f + a_b + eye_t * eye_p * d_g
    m_toep = m5.transpose(0, 3, 2, 4, 1).reshape(S5_GROUPS, S5_ROW, S5_ROW)

    pf = pw[0][1:T + 1]
    pb = pw[1][::-1][:T]

    def readout(pj):
        wc = c_c[:, None] * pj.transpose(1, 0, 2)[:, :, None, :]
        wc = wc.transpose(0, 3, 1, 2).reshape(S5_GROUPS, S5_STATE, S5_ROW)
        return jnp.concatenate([wc.real, -wc.imag], axis=1)

    c_mat = jnp.concatenate([readout(pf), readout(pb)], axis=1)

    qf = pw[0][:T][::-1]
    qb = pw[1][:T]

    def inject(qi, bb):
        wb = qi.transpose(1, 0, 2)[..., None] * bb[:, None]
        wb = wb.transpose(0, 1, 3, 2).reshape(S5_GROUPS, S5_ROW, S5_STATE)
        return jnp.concatenate([wb.real, wb.imag], axis=2)

    b_mat = jnp.concatenate([inject(qf, b_bar[0]), inject(qb, b_bar[1])], axis=2)

    a_chunk = pw[:, T]
    a1 = jnp.concatenate([a_chunk.real, a_chunk.real], axis=-1)
    a2 = jnp.concatenate([-a_chunk.imag, a_chunk.imag], axis=-1)
    return m_toep.astype(BF16), c_mat.astype(BF16), b_mat.astype(BF16), a1, a2


def _s5_inject_kernel(x_ref, b_ref, s_ref):
    for gi in range(S5_GROUP_BLOCK):
        s_ref[gi] = jnp.dot(x_ref[gi], b_ref[gi], preferred_element_type=F32)


def _s5_inject(xg, b_mat):
    bsz, _, nchunk, _ = xg.shape
    gb = S5_GROUP_BLOCK
    return pl.pallas_call(
        _s5_inject_kernel,
        grid=(bsz, S5_GROUPS // gb),
        in_specs=[
            pl.BlockSpec((None, gb, nchunk, S5_ROW), lambda b, g: (b, g, 0, 0)),
            pl.BlockSpec((gb, S5_ROW, S5_SLANES), lambda b, g: (g, 0, 0)),
        ],
        out_specs=pl.BlockSpec((None, gb, nchunk, S5_SLANES), lambda b, g: (b, g, 0, 0)),
        out_shape=jax.ShapeDtypeStruct((bsz, S5_GROUPS, nchunk, S5_SLANES), F32),
        compiler_params=_cparams("arbitrary", "arbitrary"),
        name="s5_inject",
    )(xg, b_mat)


def _s5_scan_kernel(sf_ref, sb_ref, a1_ref, a2_ref, h0_ref, hf_ref, hb_ref, hfin_ref, hf_scr, hb_scr, *, cpb):
    @pl.when(pl.program_id(0) == 0)
    def _():
        hf_scr[...] = h0_ref[0]
        hb_scr[...] = h0_ref[1]

    a1f = a1_ref[0]
    a2f = a2_ref[0]
    a1b = a1_ref[1]
    a2b = a2_ref[1]
    hf = hf_scr[...]
    hb = hb_scr[...]
    for c in range(cpb):
        hf_ref[:, c * S5_HALF:(c + 1) * S5_HALF] = hf
        hf = a1f * hf + a2f * pltpu.roll(hf, S5_STATE, 1) + sf_ref[:, c * S5_SLANES:c * S5_SLANES + S5_HALF]
        cb = cpb - 1 - c
        hb_ref[:, cb * S5_HALF:(cb + 1) * S5_HALF] = hb
        hb = (a1b * hb + a2b * pltpu.roll(hb, S5_STATE, 1)
              + sb_ref[:, cb * S5_SLANES + S5_HALF:(cb + 1) * S5_SLANES])
    hf_scr[...] = hf
    hb_scr[...] = hb

    @pl.when(pl.program_id(0) == pl.num_programs(0) - 1)
    def _():
        hfin_ref[0] = hf
        hfin_ref[1] = hb


def _s5_scan(s_loc, a1, a2, h0):
    bsz, _, nchunk, _ = s_loc.shape
    rows = bsz * S5_GROUPS
    cpb = min(S5_SCAN_BLOCK, nchunk)
    nblk = nchunk // cpb
    s2 = s_loc.reshape(rows, nchunk * S5_SLANES)
    hf, hb, hfin = pl.pallas_call(
        functools.partial(_s5_scan_kernel, cpb=cpb),
        grid=(nblk,),
        in_specs=[
            pl.BlockSpec((rows, cpb * S5_SLANES), lambda i: (0, i)),
            pl.BlockSpec((rows, cpb * S5_SLANES), lambda i: (0, nblk - 1 - i)),
            pl.BlockSpec((2, rows, S5_HALF), lambda i: (0, 0, 0)),
            pl.BlockSpec((2, rows, S5_HALF), lambda i: (0, 0, 0)),
            pl.BlockSpec((2, rows, S5_HALF), lambda i: (0, 0, 0)),
        ],
        out_specs=[
            pl.BlockSpec((rows, cpb * S5_HALF), lambda i: (0, i)),
            pl.BlockSpec((rows, cpb * S5_HALF), lambda i: (0, nblk - 1 - i)),
            pl.BlockSpec((2, rows, S5_HALF), lambda i: (0, 0, 0)),
        ],
        out_shape=[jax.ShapeDtypeStruct((rows, nchunk * S5_HALF), F32)] * 2
        + [jax.ShapeDtypeStruct((2, rows, S5_HALF), F32)],
        scratch_shapes=[pltpu.VMEM((rows, S5_HALF), F32), pltpu.VMEM((rows, S5_HALF), F32)],
        compiler_params=_cparams("arbitrary"),
        name="s5_scan",
    )(s2, s2, a1, a2, h0)
    shape4 = (bsz, S5_GROUPS, nchunk, S5_HALF)
    return hf.reshape(shape4), hb.reshape(shape4), hfin


def _s5_output_kernel(x_ref, hf_ref, hb_ref, m_ref, c_ref, y_ref):
    for gi in range(S5_GROUP_BLOCK):
        y = jnp.dot(x_ref[gi], m_ref[gi], preferred_element_type=F32)
        y = y + jnp.dot(hf_ref[gi].astype(BF16), c_ref[gi, 0:S5_HALF, :], preferred_element_type=F32)
        y_ref[gi] = y + jnp.dot(hb_ref[gi].astype(BF16), c_ref[gi, S5_HALF:2 * S5_HALF, :],
                                preferred_element_type=F32)


def _s5_output(xg, hf, hb, m_toep, c_mat):
    bsz, _, nchunk, _ = xg.shape
    gb = S5_GROUP_BLOCK
    return pl.pallas_call(
        _s5_output_kernel,
        grid=(bsz, S5_GROUPS // gb),
        in_specs=[
            pl.BlockSpec((None, gb, nchunk, S5_ROW), lambda b, g: (b, g, 0, 0)),
            pl.BlockSpec((None, gb, nchunk, S5_HALF), lambda b, g: (b, g, 0, 0)),
            pl.BlockSpec((None, gb, nchunk, S5_HALF), lambda b, g: (b, g, 0, 0)),
            pl.BlockSpec((gb, S5_ROW, S5_ROW), lambda b, g: (g, 0, 0)),
            pl.BlockSpec((gb, S5_SLANES, S5_ROW), lambda b, g: (g, 0, 0)),
        ],
        out_specs=pl.BlockSpec((None, gb, nchunk, S5_ROW), lambda b, g: (b, g, 0, 0)),
        out_shape=jax.ShapeDtypeStruct((bsz, S5_GROUPS, nchunk, S5_ROW), F32),
        compiler_params=_cparams("arbitrary", "arbitrary"),
        name="s5_output",
    )(xg, hf, hb, m_toep, c_mat)


def _s5_branch(xg_c, xg_l, mats, need_ctx):
    m_toep, c_mat, b_mat, a1, a2 = mats
    bsz = xg_l.shape[0]
    a1r = jnp.tile(a1, (1, bsz, 1))
    a2r = jnp.tile(a2, (1, bsz, 1))
    s_c = _s5_inject(xg_c, b_mat)
    s_l = _s5_inject(xg_l, b_mat)
    hf_c, hb_c, h_ctx = _s5_scan(s_c, a1r, a2r, jnp.zeros_like(a1r))
    hf_l, hb_l, _ = _s5_scan(s_l, a1r, a2r, h_ctx)
    y_l = _s5_output(xg_l, hf_l, hb_l, m_toep, c_mat)
    y_c = _s5_output(xg_c, hf_c, hb_c, m_toep, c_mat) if need_ctx else None
    return y_c, y_l


def _gla_direction(q, k, v, lr, wg, bg, tri, causal_mask, s_ref, reverse):
    z = jnp.dot(lr.astype(BF16), wg, preferred_element_type=F32) + bg
    g = (jnp.minimum(z, 0.0) - jnp.log1p(jnp.exp(-jnp.abs(z)))) * (1.0 / GLA_NORMALIZER)
    g_hi = g.astype(BF16)
    g_lo = (g - g_hi.astype(F32)).astype(BF16)
    gc = (jnp.dot(tri, g_hi, preferred_element_type=F32) + jnp.dot(tri, g_lo, preferred_element_type=F32))
    qt = q * jnp.exp(gc) * (GLA_DK ** -0.5)
    gct = gc.T
    kt = k.T
    outs = [None, None]
    order = (1, 0) if reverse else (0, 1)
    for ci in order:
        r0 = ci * GLA_CHUNK
        last = r0 if reverse else r0 + GLA_CHUNK - 1
        gct_c = gct[:, r0:r0 + GLA_CHUNK]
        kt_c = kt[:, r0:r0 + GLA_CHUNK]
        g_last = gct[:, last:last + 1]
        k_in = (kt_c * jnp.exp(-gct_c)).astype(BF16)
        k_st = (kt_c * jnp.exp(g_last - gct_c)).astype(BF16)
        dec = jnp.exp(g_last)
        qt_c = qt[r0:r0 + GLA_CHUNK].astype(BF16)
        v_c = v[r0:r0 + GLA_CHUNK].astype(BF16)
        o_heads = []
        for h in range(GLA_HEADS):
            ks = slice(h * GLA_DK, (h + 1) * GLA_DK)
            vs = slice(h * GLA_DV, (h + 1) * GLA_DV)
            q_h = qt_c[:, ks]
            v_h = v_c[:, vs]
            s_h = s_ref[ks, :]
            sc = jnp.dot(q_h, k_in[ks, :], preferred_element_type=F32)
            sc = jnp.where(causal_mask, sc, 0.0).astype(BF16)
            o_h = jnp.dot(sc, v_h, preferred_element_type=F32)
            o_h = o_h + jnp.dot(q_h, s_h.astype(BF16), preferred_element_type=F32)
            s_ref[ks, :] = dec[ks, :] * s_h + jnp.dot(k_st[ks, :], v_h, preferred_element_type=F32)
            o_heads.append(o_h)
        outs[ci] = jnp.concatenate(o_heads, axis=1)
    return jnp.concatenate(outs, axis=0)


def _gla_kernel(qf_ref, kf_ref, vf_ref, lrf_ref, qb_ref, kb_ref, vb_ref, lrb_ref, wg_ref, bg_ref, s0_ref,
                of_ref, ob_ref, sfin_ref, s_scr):
    w = pl.program_id(0)

    @pl.when(w == 0)
    def _():
        s_scr[...] = s0_ref[...]

    row = lax.broadcasted_iota(jnp.int32, (GLA_COL, GLA_COL), 0)
    col = lax.broadcasted_iota(jnp.int32, (GLA_COL, GLA_COL), 1)
    same_chunk = (row // GLA_CHUNK) == (col // GLA_CHUNK)
    tri_f = jnp.where(same_chunk & (col <= row), 1.0, 0.0).astype(BF16)
    tri_b = jnp.where(same_chunk & (col >= row), 1.0, 0.0).astype(BF16)
    r64 = lax.broadcasted_iota(jnp.int32, (GLA_CHUNK, GLA_CHUNK), 0)
    c64 = lax.broadcasted_iota(jnp.int32, (GLA_CHUNK, GLA_CHUNK), 1)

    for b in range(s_scr.shape[0]):
        of_ref[b] = _gla_direction(qf_ref[b], kf_ref[b], vf_ref[b], lrf_ref[b], wg_ref[0], bg_ref[0],
                                   tri_f, c64 <= r64, s_scr.at[b, 0], reverse=False)
        ob_ref[b] = _gla_direction(qb_ref[b], kb_ref[b], vb_ref[b], lrb_ref[b], wg_ref[1], bg_ref[1],
                                   tri_b, c64 >= r64, s_scr.at[b, 1], reverse=True)

    @pl.when(w == pl.num_programs(0) - 1)
    def _():
        sfin_ref[...] = s_scr[...]


def _gla(q, k, v, lr, wg, bg, s0):
    bsz, ncol = q.shape[:2]
    fwd = lambda w: (0, w, 0, 0)
    bwd = lambda w: (0, ncol - 1 - w, 0, 0)
    widths = (GLA_KEY, GLA_KEY, GLA_WIDTH, LR_PAD)
    state_spec = pl.BlockSpec((bsz, 2, GLA_KEY, GLA_DV), lambda w: (0, 0, 0, 0))
    in_specs = ([pl.BlockSpec((bsz, None, GLA_COL, wd), fwd) for wd in widths]
                + [pl.BlockSpec((bsz, None, GLA_COL, wd), bwd) for wd in widths]
                + [pl.BlockSpec((2, LR_PAD, GLA_KEY), lambda w: (0, 0, 0)),
                   pl.BlockSpec((2, 1, GLA_KEY), lambda w: (0, 0, 0)),
                   state_spec])
    o_shape = jax.ShapeDtypeStruct((bsz, ncol, GLA_COL, GLA_WIDTH), F32)
    return pl.pallas_call(
        _gla_kernel,
        grid=(ncol,),
        in_specs=in_specs,
        out_specs=[pl.BlockSpec((bsz, None, GLA_COL, GLA_WIDTH), fwd),
                   pl.BlockSpec((bsz, None, GLA_COL, GLA_WIDTH), bwd),
                   state_spec],
        out_shape=[o_shape, o_shape, jax.ShapeDtypeStruct((bsz, 2, GLA_KEY, GLA_DV), F32)],
        scratch_shapes=[pltpu.VMEM((bsz, 2, GLA_KEY, GLA_DV), F32)],
        compiler_params=_cparams("arbitrary"),
        name="gla",
    )(q, k, v, lr, q, k, v, lr, wg, bg, s0)


def _finish_kernel(ys_ref, zs_ref, of_ref, ob_ref, zg_ref, x_ref, mod_ref, wglu_ref, bglu_ref, gn_ref, wout_ref,
                   fin_ref, o_ref, ys_scr, o_scr, *, final, colmajor):
    nchunk = ys_ref.shape[1]
    for half in range(S5_CHUNK // SLOTS):
        for kt in range(S5_WIDTH // LANES):
            ws = [ys_ref[kt * SLOTS + qq, :, half * LANES:(half + 1) * LANES] for qq in range(SLOTS)]
            vs = _swap_slots_with_index(ws)
            for jj in range(SLOTS):
                ys_scr[kt, pl.ds(half * SLOTS + jj, nchunk, stride=CHUNK_PITCH), :] = vs[jj]
    ys = jnp.concatenate(
        [jnp.concatenate([ys_scr[kt, c * CHUNK_PITCH:c * CHUNK_PITCH + S5_CHUNK, :] for c in range(nchunk)], axis=0)
         for kt in range(S5_WIDTH // LANES)], axis=1)
    s = jax.nn.gelu(ys, approximate=True)
    t = jnp.dot(s.astype(BF16), wglu_ref[...], preferred_element_type=F32) + bglu_ref[...]
    zs = zs_ref[...]
    a_s = (s * jax.nn.sigmoid(t)) * (zs * jax.nn.sigmoid(zs))
    y = jnp.dot(a_s.astype(BF16), wout_ref[0:S5_WIDTH, :], preferred_element_type=F32)

    if colmajor:
        for w in range(GRID_W):
            o_w = of_ref[w] + ob_ref[w]
            for h in range(GLA_HEADS):
                o_scr[h, pl.ds(w, GRID_ROWS_PER_BLOCK, stride=GRID_ROW_PITCH), :] = o_w[:, h * GLA_DV:(h + 1) * GLA_DV]
        o_heads = [jnp.concatenate([o_scr[h, r * GRID_ROW_PITCH:r * GRID_ROW_PITCH + GRID_W, :]
                                    for r in range(GRID_ROWS_PER_BLOCK)], axis=0) for h in range(GLA_HEADS)]
    else:
        o_sum = of_ref[...] + ob_ref[...]
        o_heads = [o_sum[:, h * GLA_DV:(h + 1) * GLA_DV] for h in range(GLA_HEADS)]
    zg = zg_ref[...]
    gate_g = zg * jax.nn.sigmoid(zg)
    for h in range(GLA_HEADS):
        o_h = o_heads[h]
        o_n = o_h * lax.rsqrt(jnp.mean(o_h * o_h, axis=-1, keepdims=True) + EPS) * gn_ref[...]
        a_h = (o_n * gate_g[:, h * GLA_DV:(h + 1) * GLA_DV]).astype(BF16)
        y = y + jnp.dot(a_h, wout_ref[S5_WIDTH + h * GLA_DV:S5_WIDTH + (h + 1) * GLA_DV, :],
                        preferred_element_type=F32)
    gate = mod_ref[:, 2 * D_MODEL:3 * D_MODEL]
    xn = x_ref[...] + gate * y
    if final:
        ms = jnp.mean(xn * xn, axis=-1, keepdims=True)
        xn = xn * lax.rsqrt(ms + EPS) * fin_ref[...]
    o_ref[...] = xn


def _finish(ys, zs, o_f, o_b, zg, x2, mods, mod_row, w_glu, b_glu, gnorm, w_out, fin_gain, final, colmajor):
    rows = x2.shape[0]
    per_seq = rows // ys.shape[0]
    tb = min(ROW_BLOCK, per_seq)
    bps = per_seq // tb
    rb = lambda w: pl.BlockSpec((tb, w), lambda i: (i, 0))
    full = lambda a: pl.BlockSpec(a.shape, lambda i: (0,) * a.ndim)
    if colmajor:
        assert tb == ROW_BLOCK
        o_spec = pl.BlockSpec((None, GRID_W, GRID_ROWS_PER_BLOCK, GLA_WIDTH), lambda i: (i // bps, 0, i % bps, 0))
    else:
        o_spec = rb(GLA_WIDTH)
    return pl.pallas_call(
        functools.partial(_finish_kernel, final=final, colmajor=colmajor),
        grid=(rows // tb,),
        in_specs=[pl.BlockSpec((None, S5_GROUPS, tb // S5_CHUNK, S5_ROW), lambda i: (i // bps, 0, i % bps, 0)),
                  rb(S5_WIDTH), o_spec, o_spec, rb(GLA_WIDTH), rb(D_MODEL),
                  pl.BlockSpec((None, 1, 3 * D_MODEL), lambda i: (mod_row(i), 0, 0)),
                  full(w_glu), full(b_glu), full(gnorm), full(w_out), full(fin_gain)],
        out_specs=rb(D_MODEL),
        out_shape=jax.ShapeDtypeStruct((rows, D_MODEL), F32),
        scratch_shapes=[pltpu.VMEM((S5_WIDTH // LANES, tb // S5_CHUNK * CHUNK_PITCH, LANES), F32),
                        pltpu.VMEM((GLA_HEADS, GRID_ROWS_PER_BLOCK * GRID_ROW_PITCH, GLA_DV), F32)],
        compiler_params=_cparams("arbitrary"),
        name="finish",
    )(ys, zs, o_f, o_b, zg, x2, mods, w_glu, b_glu, gnorm, w_out, fin_gain)


def _layer(x_lat, x_ctx, mods, lw, need_ctx_out, final):
    bsz, l, _ = x_lat.shape
    lc = x_ctx.shape[1]
    blocks_per_batch = l // ROW_BLOCK
    lat_row = lambda i: i // blocks_per_batch
    ctx_row = lambda i: bsz

    pj_l = _inproj(x_lat.reshape(bsz * l, D_MODEL), mods, lw["gain"], lw["w_in"], lat_row, bsz, colmajor=True)
    pj_c = _inproj(x_ctx.reshape(bsz * lc, D_MODEL), mods, lw["gain"], lw["w_in"], ctx_row, bsz, colmajor=False)
    xg_l, zs_l, q_l, k_l, v_l, zg_l, lr_l = pj_l
    xg_c, zs_c, q_c, k_c, v_c, zg_c, lr_c = pj_c

    ys_c, ys_l = _s5_branch(xg_c, xg_l, lw["s5"], need_ctx_out)

    cols = lambda t: t.reshape(bsz, lc // GLA_COL, GLA_COL, t.shape[-1])
    s_zero = jnp.zeros((bsz, 2, GLA_KEY, GLA_DV), F32)
    ocf, ocb, s_ctx = _gla(cols(q_c), cols(k_c), cols(v_c), cols(lr_c), lw["wg"], lw["bg"], s_zero)
    olf, olb, _ = _gla(q_l, k_l, v_l, lr_l, lw["wg"], lw["bg"], s_ctx)

    fin_args = (lw["w_glu"], lw["b_glu"], lw["gnorm"], lw["w_out"], lw["fin"])
    x_lat_new = _finish(ys_l, zs_l, olf, olb, zg_l, x_lat.reshape(bsz * l, D_MODEL), mods, lat_row,
                        *fin_args, final=final, colmajor=True).reshape(bsz, l, D_MODEL)
    x_ctx_new = x_ctx
    if need_ctx_out:
        x_ctx_new = _finish(ys_c, zs_c, ocf.reshape(bsz * lc, GLA_WIDTH), ocb.reshape(bsz * lc, GLA_WIDTH), zg_c,
                            x_ctx.reshape(bsz * lc, D_MODEL), mods, ctx_row, *fin_args, final=False,
                            colmajor=False).reshape(bsz, lc, D_MODEL)
    return x_lat_new, x_ctx_new


def _stack_weights(norm_g, w_in, s5_lam_re, s5_lam_im, s5_log_dt, s5_b_re, s5_b_im, s5_c_re, s5_c_im, s5_d,
                   s5_w_glu, s5_b_glu, gla_w_gate, gla_b_gate, gla_norm_g, w_out, final_norm):
    w_pad = jnp.pad(w_in, ((0, 0), (0, 0), (0, IN_PAD - w_in.shape[-1]))).astype(BF16)
    wg = jnp.zeros((DEPTH, 2, LR_PAD, GLA_KEY), F32)
    for d in range(2):
        wg = wg.at[:, d, d * GLA_RANK:(d + 1) * GLA_RANK].set(gla_w_gate[:, d])
    s5 = jax.vmap(_s5_matrices)(s5_lam_re, s5_lam_im, s5_log_dt, s5_b_re, s5_b_im, s5_c_re, s5_c_im, s5_d)
    wg = wg.astype(BF16)
    w_glu = s5_w_glu.astype(BF16)
    w_o = w_out.astype(BF16)
    return [{
        "gain": norm_g[i].reshape(1, D_MODEL),
        "w_in": w_pad[i],
        "s5": tuple(m[i] for m in s5),
        "wg": wg[i],
        "bg": gla_b_gate[i].reshape(2, 1, GLA_KEY),
        "w_glu": w_glu[i],
        "b_glu": s5_b_glu[i].reshape(1, S5_WIDTH),
        "gnorm": gla_norm_g[i].reshape(1, GLA_DV),
        "w_out": w_o[i],
        "fin": final_norm.reshape(1, D_MODEL),
    } for i in range(DEPTH)]


def kernel(x, c, ctx, c_ctx, norm_g, w_mod, b_mod, w_in, s5_lam_re, s5_lam_im, s5_log_dt, s5_b_re, s5_b_im,
           s5_c_re, s5_c_im, s5_d, s5_w_glu, s5_b_glu, gla_w_gate, gla_b_gate, gla_norm_g, w_out, final_norm):
    bsz = x.shape[0]
    cond8 = jnp.concatenate([c, c_ctx[None], jnp.zeros((8 - bsz - 1, D_MODEL), F32)], axis=0)
    mods_all = _modulation(cond8, w_mod, b_mod)
    weights = _stack_weights(norm_g, w_in, s5_lam_re, s5_lam_im, s5_log_dt, s5_b_re, s5_b_im, s5_c_re, s5_c_im,
                             s5_d, s5_w_glu, s5_b_glu, gla_w_gate, gla_b_gate, gla_norm_g, w_out, final_norm)
    x_lat, x_ctx = x, ctx
    for i in range(DEPTH):
        mods = mods_all[i].reshape(8, 1, 3 * D_MODEL)
        last = i == DEPTH - 1
        x_lat, x_ctx = _layer(x_lat, x_ctx, mods, weights[i], need_ctx_out=not last, final=last)
    return x_lat
```

```python
import functools

import jax
import jax.numpy as jnp
from jax import lax
from jax.experimental import pallas as pl
from jax.experimental.pallas import tpu as pltpu

F32 = jnp.float32
BF16 = jnp.bfloat16

D_MODEL = 1024
DEPTH = 4
GRID_W = 64
EPS = 1e-6
LANES = 128
SUBLANES = 8
S5_WIDTH = 512
S5_GROUP = 16
S5_GROUPS = 32
S5_STATE = 64
S5_CHUNK = 16
S5_ROW = S5_CHUNK * S5_GROUP
S5_HALF = 2 * S5_STATE
S5_SLANES = 2 * S5_HALF
S5_SCAN_BLOCK = 32
S5_GROUP_BLOCK = 8
SLOTS = LANES // S5_GROUP
GLA_HEADS = 4
GLA_DV = 128
GLA_DK = 64
GLA_WIDTH = 512
GLA_KEY = 256
GLA_RANK = 16
GLA_NORMALIZER = 16.0
GLA_CHUNK = 64
GLA_COL = 128
LR_PAD = 128
IN_PAD = 2 * S5_WIDTH + 2 * GLA_KEY + 2 * GLA_WIDTH + LR_PAD

ROW_BLOCK = 512
GRID_ROWS_PER_BLOCK = ROW_BLOCK // GRID_W
CHUNK_PITCH = S5_CHUNK + SUBLANES
GRID_ROW_PITCH = GRID_W + SUBLANES
VMEM_LIMIT = 48 * 1024 * 1024


def _cparams(*sem):
    return pltpu.CompilerParams(dimension_semantics=sem, vmem_limit_bytes=VMEM_LIMIT)


def _mod_kernel(cond_ref, w_ref, b_ref, o_ref):
    cnd = cond_ref[...]
    a = cnd * jax.nn.sigmoid(cnd)
    o_ref[...] = jnp.dot(a, w_ref[...], preferred_element_type=F32,
                         precision=lax.Precision.HIGHEST) + b_ref[...]


def _modulation(cond8, w_mod, b_mod):
    nblk = 3
    return pl.pallas_call(
        _mod_kernel,
        grid=(DEPTH, nblk),
        in_specs=[
            pl.BlockSpec((8, D_MODEL), lambda l, j: (0, 0)),
            pl.BlockSpec((None, D_MODEL, D_MODEL), lambda l, j: (l, 0, j)),
            pl.BlockSpec((None, 1, D_MODEL), lambda l, j: (l, 0, j)),
        ],
        out_specs=pl.BlockSpec((None, 8, D_MODEL), lambda l, j: (l, 0, j)),
        out_shape=jax.ShapeDtypeStruct((DEPTH, 8, 3 * D_MODEL), F32),
        compiler_params=_cparams("arbitrary", "arbitrary"),
        name="modulation",
    )(cond8, w_mod, b_mod.reshape(DEPTH, 1, 3 * D_MODEL))


_IN_SEGS = (("zs", S5_WIDTH, False), ("q", GLA_KEY, True), ("k", GLA_KEY, True), ("v", GLA_WIDTH, True),
            ("zg", GLA_WIDTH, False), ("lr", LR_PAD, True))


def _swap_slots_with_index(vs):
    lane = lax.broadcasted_iota(jnp.int32, vs[0].shape, 1)
    for d in (4, 2, 1):
        sh = d * S5_GROUP
        low = (lane & sh) == 0
        nxt = list(vs)
        for i in range(SLOTS):
            if i & d == 0:
                a, b = vs[i], vs[i + d]
                nxt[i] = jnp.where(low, a, pltpu.roll(b, sh, 1))
                nxt[i + d] = jnp.where(low, pltpu.roll(a, LANES - sh, 1), b)
        vs = nxt
    return vs


def _inproj_kernel(x_ref, mod_ref, gain_ref, w_ref, xg_ref, *rest, colmajor):
    out_refs, u_scr, c_scr = rest[:-2], rest[-2], rest[-1]
    x = x_ref[...]
    ms = jnp.mean(x * x, axis=-1, keepdims=True)
    y = x * lax.rsqrt(ms + EPS) * gain_ref[...]
    shift = mod_ref[:, 0:D_MODEL]
    scale = mod_ref[:, D_MODEL:2 * D_MODEL]
    h = (y * (1.0 + scale) + shift).astype(BF16)
    u = jnp.dot(h, w_ref[:, 0:S5_WIDTH], preferred_element_type=F32)
    nchunk = u.shape[0] // S5_CHUNK
    for kt in range(S5_WIDTH // LANES):
        for c in range(nchunk):
            u_scr[kt, c * CHUNK_PITCH:c * CHUNK_PITCH + S5_CHUNK, :] = u[c * S5_CHUNK:(c + 1) * S5_CHUNK,
                                                                         kt * LANES:(kt + 1) * LANES]
    off = S5_WIDTH
    for (_, width, cm), o_ref in zip(_IN_SEGS, out_refs):
        val = jnp.dot(h, w_ref[:, off:off + width], preferred_element_type=F32)
        off += width
        if colmajor and cm:
            for kt in range(width // LANES):
                for r in range(GRID_ROWS_PER_BLOCK):
                    c_scr[kt, r * GRID_ROW_PITCH:r * GRID_ROW_PITCH + GRID_W, :] = val[r * GRID_W:(r + 1) * GRID_W,
                                                                                       kt * LANES:(kt + 1) * LANES]
            for w in range(GRID_W):
                for kt in range(width // LANES):
                    o_ref[w, :, kt * LANES:(kt + 1) * LANES] = c_scr[kt, pl.ds(w, GRID_ROWS_PER_BLOCK,
                                                                              stride=GRID_ROW_PITCH), :]
        else:
            o_ref[...] = val
    for half in range(S5_CHUNK // SLOTS):
        for kt in range(S5_WIDTH // LANES):
            vs = [u_scr[kt, pl.ds(half * SLOTS + jj, nchunk, stride=CHUNK_PITCH), :] for jj in range(SLOTS)]
            ws = _swap_slots_with_index(vs)
            for qq in range(SLOTS):
                xg_ref[kt * SLOTS + qq, :, half * LANES:(half + 1) * LANES] = ws[qq].astype(BF16)


def _inproj(x2, mods, gain, w_pad, mod_row, bsz, colmajor):
    rows = x2.shape[0]
    per_seq = rows // bsz
    tb = min(ROW_BLOCK, per_seq)
    bps = per_seq // tb
    out_specs = [pl.BlockSpec((None, S5_GROUPS, tb // S5_CHUNK, S5_ROW), lambda i: (i // bps, 0, i % bps, 0))]
    out_shape = [jax.ShapeDtypeStruct((bsz, S5_GROUPS, per_seq // S5_CHUNK, S5_ROW), BF16)]
    for _, w, cm in _IN_SEGS:
        if colmajor and cm:
            assert tb == ROW_BLOCK
            out_specs.append(pl.BlockSpec((None, GRID_W, GRID_ROWS_PER_BLOCK, w),
                                          lambda i: (i // bps, 0, i % bps, 0)))
            out_shape.append(jax.ShapeDtypeStruct((bsz, GRID_W, per_seq // GRID_W, w), F32))
        else:
            out_specs.append(pl.BlockSpec((tb, w), lambda i: (i, 0)))
            out_shape.append(jax.ShapeDtypeStruct((rows, w), F32))
    return pl.pallas_call(
        functools.partial(_inproj_kernel, colmajor=colmajor),
        grid=(rows // tb,),
        in_specs=[
            pl.BlockSpec((tb, D_MODEL), lambda i: (i, 0)),
            pl.BlockSpec((None, 1, 3 * D_MODEL), lambda i: (mod_row(i), 0, 0)),
            pl.BlockSpec((1, D_MODEL), lambda i: (0, 0)),
            pl.BlockSpec((D_MODEL, IN_PAD), lambda i: (0, 0)),
        ],
        out_specs=out_specs,
        out_shape=out_shape,
        scratch_shapes=[pltpu.VMEM((S5_WIDTH // LANES, tb // S5_CHUNK * CHUNK_PITCH, LANES), F32),
                        pltpu.VMEM((GLA_WIDTH // LANES, GRID_ROWS_PER_BLOCK * GRID_ROW_PITCH, LANES), F32)],
        compiler_params=_cparams("arbitrary"),
        name="inproj",
    )(x2, mods, gain, w_pad)


def _s5_matrices(lam_re, lam_im, log_dt, b_re, b_im, c_re, c_im, d_skip):
    hp = lax.Precision.HIGHEST
    T = S5_CHUNK
    lam = lax.complex(lam_re.astype(F32), lam_im.astype(F32))
    dt = jnp.exp(log_dt.astype(F32))[..., None]
    ldt = lam * dt
    lam_bar = jnp.exp(ldt)
    b_c = lax.complex(b_re.astype(F32), b_im.astype(F32))
    b_bar = ((lam_bar - 1.0) / lam)[..., None] * b_c[None]
    c_c = lax.complex(c_re.astype(F32), c_im.astype(F32))
    steps = jnp.arange(T + 1, dtype=F32)
    pw = jnp.exp(ldt[:, None] * steps[None, :, None, None].astype(jnp.complex64))

    kern = jnp.einsum('gpn,dtgn,dgnq->dgqtp', c_c, pw[:, :T], b_bar, precision=hp).real
    skip = jnp.eye(S5_GROUP, dtype=F32)[None] * d_skip.astype(F32).reshape(S5_GROUPS, 1, S5_GROUP)
    k_f = kern[0].at[:, :, 0, :].add(skip).reshape(S5_GROUPS, S5_GROUP, S5_ROW)
    k_b = kern[1][:, :, ::-1].reshape(S5_GROUPS, S5_GROUP, S5_ROW)
    zero_rows = jnp.zeros_like(k_f)
    toep_rows = jnp.stack([jnp.concatenate([zero_rows, k_f], axis=-1),
                           jnp.concatenate([k_b, zero_rows], axis=-1)], axis=1)

    pf = pw[0][1:T + 1]
    pb = pw[1][::-1][:T]

    def readout(pj):
        wc = c_c[:, None] * pj.transpose(1, 0, 2)[:, :, None, :]
        wc = wc.transpose(0, 3, 1, 2).reshape(S5_GROUPS, S5_STATE, S5_ROW)
        return jnp.concatenate([wc.real, -wc.imag], axis=1)

    c_mat = jnp.concatenate([readout(pf), readout(pb)], axis=1)

    qf = pw[0][:T][::-1]
    qb = pw[1][:T]

    def inject(qi, bb):
        wb = qi.transpose(1, 0, 2)[..., None] * bb[:, None]
        wb = wb.transpose(0, 1, 3, 2).reshape(S5_GROUPS, S5_ROW, S5_STATE)
        return jnp.concatenate([wb.real, wb.imag], axis=2)

    b_mat = jnp.concatenate([inject(qf, b_bar[0]), inject(qb, b_bar[1])], axis=2)

    a_chunk = pw[:, T]
    a1 = jnp.concatenate([a_chunk.real, a_chunk.real], axis=-1)
    a2 = jnp.concatenate([-a_chunk.imag, a_chunk.imag], axis=-1)
    return toep_rows, c_mat.astype(BF16), b_mat.astype(BF16), a1, a2


def _s5_inject_kernel(x_ref, b_ref, s_ref):
    for gi in range(S5_GROUP_BLOCK):
        s_ref[gi] = jnp.dot(x_ref[gi], b_ref[gi], preferred_element_type=F32)


def _s5_inject(xg, b_mat):
    bsz, _, nchunk, _ = xg.shape
    gb = S5_GROUP_BLOCK
    return pl.pallas_call(
        _s5_inject_kernel,
        grid=(bsz, S5_GROUPS // gb),
        in_specs=[
            pl.BlockSpec((None, gb, nchunk, S5_ROW), lambda b, g: (b, g, 0, 0)),
            pl.BlockSpec((gb, S5_ROW, S5_SLANES), lambda b, g: (g, 0, 0)),
        ],
        out_specs=pl.BlockSpec((None, gb, nchunk, S5_SLANES), lambda b, g: (b, g, 0, 0)),
        out_shape=jax.ShapeDtypeStruct((bsz, S5_GROUPS, nchunk, S5_SLANES), F32),
        compiler_params=_cparams("arbitrary", "arbitrary"),
        name="s5_inject",
    )(xg, b_mat)


def _s5_scan_kernel(sf_ref, sb_ref, a1_ref, a2_ref, h0_ref, hf_ref, hb_ref, hfin_ref, hf_scr, hb_scr, *, cpb):
    @pl.when(pl.program_id(0) == 0)
    def _():
        hf_scr[...] = h0_ref[0]
        hb_scr[...] = h0_ref[1]

    a1f = a1_ref[0]
    a2f = a2_ref[0]
    a1b = a1_ref[1]
    a2b = a2_ref[1]
    hf = hf_scr[...]
    hb = hb_scr[...]
    for c in range(cpb):
        hf_ref[:, c * S5_HALF:(c + 1) * S5_HALF] = hf
        hf = a1f * hf + a2f * pltpu.roll(hf, S5_STATE, 1) + sf_ref[:, c * S5_SLANES:c * S5_SLANES + S5_HALF]
        cb = cpb - 1 - c
        hb_ref[:, cb * S5_HALF:(cb + 1) * S5_HALF] = hb
        hb = (a1b * hb + a2b * pltpu.roll(hb, S5_STATE, 1)
              + sb_ref[:, cb * S5_SLANES + S5_HALF:(cb + 1) * S5_SLANES])
    hf_scr[...] = hf
    hb_scr[...] = hb

    @pl.when(pl.program_id(0) == pl.num_programs(0) - 1)
    def _():
        hfin_ref[0] = hf
        hfin_ref[1] = hb


def _s5_scan(s_loc, a1, a2, h0):
    bsz, _, nchunk, _ = s_loc.shape
    rows = bsz * S5_GROUPS
    cpb = min(S5_SCAN_BLOCK, nchunk)
    nblk = nchunk // cpb
    s2 = s_loc.reshape(rows, nchunk * S5_SLANES)
    hf, hb, hfin = pl.pallas_call(
        functools.partial(_s5_scan_kernel, cpb=cpb),
        grid=(nblk,),
        in_specs=[
            pl.BlockSpec((rows, cpb * S5_SLANES), lambda i: (0, i)),
            pl.BlockSpec((rows, cpb * S5_SLANES), lambda i: (0, nblk - 1 - i)),
            pl.BlockSpec((2, rows, S5_HALF), lambda i: (0, 0, 0)),
            pl.BlockSpec((2, rows, S5_HALF), lambda i: (0, 0, 0)),
            pl.BlockSpec((2, rows, S5_HALF), lambda i: (0, 0, 0)),
        ],
        out_specs=[
            pl.BlockSpec((rows, cpb * S5_HALF), lambda i: (0, i)),
            pl.BlockSpec((rows, cpb * S5_HALF), lambda i: (0, nblk - 1 - i)),
            pl.BlockSpec((2, rows, S5_HALF), lambda i: (0, 0, 0)),
        ],
        out_shape=[jax.ShapeDtypeStruct((rows, nchunk * S5_HALF), F32)] * 2
        + [jax.ShapeDtypeStruct((2, rows, S5_HALF), F32)],
        scratch_shapes=[pltpu.VMEM((rows, S5_HALF), F32), pltpu.VMEM((rows, S5_HALF), F32)],
        compiler_params=_cparams("arbitrary"),
        name="s5_scan",
    )(s2, s2, a1, a2, h0)
    shape4 = (bsz, S5_GROUPS, nchunk, S5_HALF)
    return hf.reshape(shape4), hb.reshape(shape4), hfin


def _s5_output_kernel(x_ref, hf_ref, hb_ref, t_ref, c_ref, y_ref, m_scr):
    for gi in range(S5_GROUP_BLOCK):
        k_f = t_ref[gi, 0]
        k_b = t_ref[gi, 1]
        for i in range(S5_CHUNK):
            lo_f = S5_ROW - i * S5_GROUP
            lo_b = (S5_CHUNK - 1 - i) * S5_GROUP
            m_scr[gi, i * S5_GROUP:(i + 1) * S5_GROUP, :] = (k_f[:, lo_f:lo_f + S5_ROW]
                                                             + k_b[:, lo_b:lo_b + S5_ROW]).astype(BF16)
        y = jnp.dot(x_ref[gi], m_scr[gi], preferred_element_type=F32)
        y = y + jnp.dot(hf_ref[gi].astype(BF16), c_ref[gi, 0:S5_HALF, :], preferred_element_type=F32)
        y_ref[gi] = y + jnp.dot(hb_ref[gi].astype(BF16), c_ref[gi, S5_HALF:2 * S5_HALF, :],
                                preferred_element_type=F32)


def _s5_output(xg, hf, hb, toep_rows, c_mat):
    bsz, _, nchunk, _ = xg.shape
    gb = S5_GROUP_BLOCK
    return pl.pallas_call(
        _s5_output_kernel,
        grid=(bsz, S5_GROUPS // gb),
        in_specs=[
            pl.BlockSpec((None, gb, nchunk, S5_ROW), lambda b, g: (b, g, 0, 0)),
            pl.BlockSpec((None, gb, nchunk, S5_HALF), lambda b, g: (b, g, 0, 0)),
            pl.BlockSpec((None, gb, nchunk, S5_HALF), lambda b, g: (b, g, 0, 0)),
            pl.BlockSpec((gb, 2, S5_GROUP, 2 * S5_ROW), lambda b, g: (g, 0, 0, 0)),
            pl.BlockSpec((gb, S5_SLANES, S5_ROW), lambda b, g: (g, 0, 0)),
        ],
        out_specs=pl.BlockSpec((None, gb, nchunk, S5_ROW), lambda b, g: (b, g, 0, 0)),
        out_shape=jax.ShapeDtypeStruct((bsz, S5_GROUPS, nchunk, S5_ROW), F32),
        scratch_shapes=[pltpu.VMEM((gb, S5_ROW, S5_ROW), BF16)],
        compiler_params=_cparams("arbitrary", "arbitrary"),
        name="s5_output",
    )(xg, hf, hb, toep_rows, c_mat)


def _s5_branch(xg_c, xg_l, mats, need_ctx):
    toep_rows, c_mat, b_mat, a1, a2 = mats
    bsz = xg_l.shape[0]
    a1r = jnp.tile(a1, (1, bsz, 1))
    a2r = jnp.tile(a2, (1, bsz, 1))
    s_c = _s5_inject(xg_c, b_mat)
    s_l = _s5_inject(xg_l, b_mat)
    hf_c, hb_c, h_ctx = _s5_scan(s_c, a1r, a2r, jnp.zeros_like(a1r))
    hf_l, hb_l, _ = _s5_scan(s_l, a1r, a2r, h_ctx)
    y_l = _s5_output(xg_l, hf_l, hb_l, toep_rows, c_mat)
    y_c = _s5_output(xg_c, hf_c, hb_c, toep_rows, c_mat) if need_ctx else None
    return y_c, y_l


def _gla_kernel(qf_ref, kf_ref, vf_ref, lrf_ref, qb_ref, kb_ref, vb_ref, lrb_ref, wg_ref, bg_ref, s0_ref,
                of_ref, ob_ref, sfin_ref, s_scr):
    w = pl.program_id(0)

    @pl.when(w == 0)
    def _():
        s_scr[...] = s0_ref[...]

    row = lax.broadcasted_iota(jnp.int32, (GLA_COL, GLA_COL), 0)
    col = lax.broadcasted_iota(jnp.int32, (GLA_COL, GLA_COL), 1)
    same_chunk = (row // GLA_CHUNK) == (col // GLA_CHUNK)
    r64 = lax.broadcasted_iota(jnp.int32, (GLA_CHUNK, GLA_CHUNK), 0)
    c64 = lax.broadcasted_iota(jnp.int32, (GLA_CHUNK, GLA_CHUNK), 1)
    dirs = (
        (qf_ref, kf_ref, vf_ref, lrf_ref, of_ref, jnp.where(same_chunk & (col <= row), 1.0, 0.0).astype(BF16),
         c64 <= r64),
        (qb_ref, kb_ref, vb_ref, lrb_ref, ob_ref, jnp.where(same_chunk & (col >= row), 1.0, 0.0).astype(BF16),
         c64 >= r64),
    )
    chains = [(b, d) for b in range(s_scr.shape[0]) for d in range(2)]

    pre = []
    for b, d in chains:
        q_ref, k_ref, _, lr_ref, _, tri, _ = dirs[d]
        z = jnp.dot(lr_ref[b].astype(BF16), wg_ref[d], preferred_element_type=F32) + bg_ref[d]
        g = (jnp.minimum(z, 0.0) - jnp.log1p(jnp.exp(-jnp.abs(z)))) * (1.0 / GLA_NORMALIZER)
        g_hi = g.astype(BF16)
        g_lo = (g - g_hi.astype(F32)).astype(BF16)
        gc = jnp.dot(tri, g_hi, preferred_element_type=F32) + jnp.dot(tri, g_lo, preferred_element_type=F32)
        qt = (q_ref[b] * jnp.exp(gc) * (GLA_DK ** -0.5)).astype(BF16)
        pre.append((qt, gc.T, k_ref[b].T))

    for pos in range(2):
        keys = []
        for (b, d), (qt, gct, kt) in zip(chains, pre):
            ci = pos if d == 0 else 1 - pos
            r0 = ci * GLA_CHUNK
            last = r0 + GLA_CHUNK - 1 if d == 0 else r0
            gct_c = gct[:, r0:r0 + GLA_CHUNK]
            dec = jnp.exp(gct[:, last:last + 1])
            k_in = kt[:, r0:r0 + GLA_CHUNK] * jnp.exp(-gct_c)
            keys.append((r0, dec, k_in.astype(BF16), (k_in * dec).astype(BF16)))
        for h in range(GLA_HEADS):
            ks = slice(h * GLA_DK, (h + 1) * GLA_DK)
            vs = slice(h * GLA_DV, (h + 1) * GLA_DV)
            scores = []
            for (b, d), (qt, _, _), (r0, _, k_in, _) in zip(chains, pre, keys):
                sc = jnp.dot(qt[r0:r0 + GLA_CHUNK, ks], k_in[ks, :], preferred_element_type=F32)
                scores.append(jnp.where(dirs[d][6], sc, 0.0).astype(BF16))
            for (b, d), (qt, _, _), (r0, dec, _, k_st), sc in zip(chains, pre, keys, scores):
                v_h = dirs[d][2][b, r0:r0 + GLA_CHUNK, vs].astype(BF16)
                s_h = s_scr[b, d, ks, :]
                o_h = jnp.dot(sc, v_h, preferred_element_type=F32)
                o_h = o_h + jnp.dot(qt[r0:r0 + GLA_CHUNK, ks], s_h.astype(BF16), preferred_element_type=F32)
                dirs[d][4][b, r0:r0 + GLA_CHUNK, vs] = o_h
                s_scr[b, d, ks, :] = dec[ks, :] * s_h + jnp.dot(k_st[ks, :], v_h, preferred_element_type=F32)

    @pl.when(w == pl.num_programs(0) - 1)
    def _():
        sfin_ref[...] = s_scr[...]


def _gla(q, k, v, lr, wg, bg, s0):
    bsz, ncol = q.shape[:2]
    fwd = lambda w: (0, w, 0, 0)
    bwd = lambda w: (0, ncol - 1 - w, 0, 0)
    widths = (GLA_KEY, GLA_KEY, GLA_WIDTH, LR_PAD)
    state_spec = pl.BlockSpec((bsz, 2, GLA_KEY, GLA_DV), lambda w: (0, 0, 0, 0))
    in_specs = ([pl.BlockSpec((bsz, None, GLA_COL, wd), fwd) for wd in widths]
                + [pl.BlockSpec((bsz, None, GLA_COL, wd), bwd) for wd in widths]
                + [pl.BlockSpec((2, LR_PAD, GLA_KEY), lambda w: (0, 0, 0)),
                   pl.BlockSpec((2, 1, GLA_KEY), lambda w: (0, 0, 0)),
                   state_spec])
    o_shape = jax.ShapeDtypeStruct((bsz, ncol, GLA_COL, GLA_WIDTH), F32)
    return pl.pallas_call(
        _gla_kernel,
        grid=(ncol,),
        in_specs=in_specs,
        out_specs=[pl.BlockSpec((bsz, None, GLA_COL, GLA_WIDTH), fwd),
                   pl.BlockSpec((bsz, None, GLA_COL, GLA_WIDTH), bwd),
                   state_spec],
        out_shape=[o_shape, o_shape, jax.ShapeDtypeStruct((bsz, 2, GLA_KEY, GLA_DV), F32)],
        scratch_shapes=[pltpu.VMEM((bsz, 2, GLA_KEY, GLA_DV), F32)],
        compiler_params=_cparams("arbitrary"),
        name="gla",
    )(q, k, v, lr, q, k, v, lr, wg, bg, s0)


def _finish_kernel(ys_ref, zs_ref, of_ref, ob_ref, zg_ref, x_ref, mod_ref, wglu_ref, bglu_ref, gn_ref, wout_ref,
                   fin_ref, o_ref, ys_scr, o_scr, *, final, colmajor):
    nchunk = ys_ref.shape[1]
    for half in range(S5_CHUNK // SLOTS):
        for kt in range(S5_WIDTH // LANES):
            ws = [ys_ref[kt * SLOTS + qq, :, half * LANES:(half + 1) * LANES] for qq in range(SLOTS)]
            vs = _swap_slots_with_index(ws)
            for jj in range(SLOTS):
                ys_scr[kt, pl.ds(half * SLOTS + jj, nchunk, stride=CHUNK_PITCH), :] = vs[jj]
    ys = jnp.concatenate(
        [jnp.concatenate([ys_scr[kt, c * CHUNK_PITCH:c * CHUNK_PITCH + S5_CHUNK, :] for c in range(nchunk)], axis=0)
         for kt in range(S5_WIDTH // LANES)], axis=1)
    s = jax.nn.gelu(ys, approximate=True)
    t = jnp.dot(s.astype(BF16), wglu_ref[...], preferred_element_type=F32) + bglu_ref[...]
    zs = zs_ref[...]
    a_s = (s * jax.nn.sigmoid(t)) * (zs * jax.nn.sigmoid(zs))
    y = jnp.dot(a_s.astype(BF16), wout_ref[0:S5_WIDTH, :], preferred_element_type=F32)

    if colmajor:
        for w in range(GRID_W):
            o_w = of_ref[w] + ob_ref[w]
            for h in range(GLA_HEADS):
                o_scr[h, pl.ds(w, GRID_ROWS_PER_BLOCK, stride=GRID_ROW_PITCH), :] = o_w[:, h * GLA_DV:(h + 1) * GLA_DV]
        o_heads = [jnp.concatenate([o_scr[h, r * GRID_ROW_PITCH:r * GRID_ROW_PITCH + GRID_W, :]
                                    for r in range(GRID_ROWS_PER_BLOCK)], axis=0) for h in range(GLA_HEADS)]
    else:
        o_sum = of_ref[...] + ob_ref[...]
        o_heads = [o_sum[:, h * GLA_DV:(h + 1) * GLA_DV] for h in range(GLA_HEADS)]
    zg = zg_ref[...]
    gate_g = zg * jax.nn.sigmoid(zg)
    for h in range(GLA_HEADS):
        o_h = o_heads[h]
        o_n = o_h * lax.rsqrt(jnp.mean(o_h * o_h, axis=-1, keepdims=True) + EPS) * gn_ref[...]
        a_h = (o_n * gate_g[:, h * GLA_DV:(h + 1) * GLA_DV]).astype(BF16)
        y = y + jnp.dot(a_h, wout_ref[S5_WIDTH + h * GLA_DV:S5_WIDTH + (h + 1) * GLA_DV, :],
                        preferred_element_type=F32)
    gate = mod_ref[:, 2 * D_MODEL:3 * D_MODEL]
    xn = x_ref[...] + gate * y
    if final:
        ms = jnp.mean(xn * xn, axis=-1, keepdims=True)
        xn = xn * lax.rsqrt(ms + EPS) * fin_ref[...]
    o_ref[...] = xn


def _finish(ys, zs, o_f, o_b, zg, x2, mods, mod_row, w_glu, b_glu, gnorm, w_out, fin_gain, final, colmajor):
    rows = x2.shape[0]
    per_seq = rows // ys.shape[0]
    tb = min(ROW_BLOCK, per_seq)
    bps = per_seq // tb
    rb = lambda w: pl.BlockSpec((tb, w), lambda i: (i, 0))
    full = lambda a: pl.BlockSpec(a.shape, lambda i: (0,) * a.ndim)
    if colmajor:
        assert tb == ROW_BLOCK
        o_spec = pl.BlockSpec((None, GRID_W, GRID_ROWS_PER_BLOCK, GLA_WIDTH), lambda i: (i // bps, 0, i % bps, 0))
    else:
        o_spec = rb(GLA_WIDTH)
    return pl.pallas_call(
        functools.partial(_finish_kernel, final=final, colmajor=colmajor),
        grid=(rows // tb,),
        in_specs=[pl.BlockSpec((None, S5_GROUPS, tb // S5_CHUNK, S5_ROW), lambda i: (i // bps, 0, i % bps, 0)),
                  rb(S5_WIDTH), o_spec, o_spec, rb(GLA_WIDTH), rb(D_MODEL),
                  pl.BlockSpec((None, 1, 3 * D_MODEL), lambda i: (mod_row(i), 0, 0)),
                  full(w_glu), full(b_glu), full(gnorm), full(w_out), full(fin_gain)],
        out_specs=rb(D_MODEL),
        out_shape=jax.ShapeDtypeStruct((rows, D_MODEL), F32),
        scratch_shapes=[pltpu.VMEM((S5_WIDTH // LANES, tb // S5_CHUNK * CHUNK_PITCH, LANES), F32),
                        pltpu.VMEM((GLA_HEADS, GRID_ROWS_PER_BLOCK * GRID_ROW_PITCH, GLA_DV), F32)],
        compiler_params=_cparams("arbitrary"),
        name="finish",
    )(ys, zs, o_f, o_b, zg, x2, mods, w_glu, b_glu, gnorm, w_out, fin_gain)


def _layer(x_lat, x_ctx, mods, lw, need_ctx_out, final):
    bsz, l, _ = x_lat.shape
    lc = x_ctx.shape[1]
    blocks_per_batch = l // ROW_BLOCK
    lat_row = lambda i: i // blocks_per_batch
    ctx_row = lambda i: bsz

    pj_l = _inproj(x_lat.reshape(bsz * l, D_MODEL), mods, lw["gain"], lw["w_in"], lat_row, bsz, colmajor=True)
    pj_c = _inproj(x_ctx.reshape(bsz * lc, D_MODEL), mods, lw["gain"], lw["w_in"], ctx_row, bsz, colmajor=False)
    xg_l, zs_l, q_l, k_l, v_l, zg_l, lr_l = pj_l
    xg_c, zs_c, q_c, k_c, v_c, zg_c, lr_c = pj_c

    ys_c, ys_l = _s5_branch(xg_c, xg_l, lw["s5"], need_ctx_out)

    cols = lambda t: t.reshape(bsz, lc // GLA_COL, GLA_COL, t.shape[-1])
    s_zero = jnp.zeros((bsz, 2, GLA_KEY, GLA_DV), F32)
    ocf, ocb, s_ctx = _gla(cols(q_c), cols(k_c), cols(v_c), cols(lr_c), lw["wg"], lw["bg"], s_zero)
    olf, olb, _ = _gla(q_l, k_l, v_l, lr_l, lw["wg"], lw["bg"], s_ctx)

    fin_args = (lw["w_glu"], lw["b_glu"], lw["gnorm"], lw["w_out"], lw["fin"])
    x_lat_new = _finish(ys_l, zs_l, olf, olb, zg_l, x_lat.reshape(bsz * l, D_MODEL), mods, lat_row,
                        *fin_args, final=final, colmajor=True).reshape(bsz, l, D_MODEL)
    x_ctx_new = x_ctx
    if need_ctx_out:
        x_ctx_new = _finish(ys_c, zs_c, ocf.reshape(bsz * lc, GLA_WIDTH), ocb.reshape(bsz * lc, GLA_WIDTH), zg_c,
                            x_ctx.reshape(bsz * lc, D_MODEL), mods, ctx_row, *fin_args, final=False,
                            colmajor=False).reshape(bsz, lc, D_MODEL)
    return x_lat_new, x_ctx_new


def _stack_weights(norm_g, w_in, s5_lam_re, s5_lam_im, s5_log_dt, s5_b_re, s5_b_im, s5_c_re, s5_c_im, s5_d,
                   s5_w_glu, s5_b_glu, gla_w_gate, gla_b_gate, gla_norm_g, w_out, final_norm):
    w_pad = jnp.pad(w_in, ((0, 0), (0, 0), (0, IN_PAD - w_in.shape[-1]))).astype(BF16)
    wg = jnp.zeros((DEPTH, 2, LR_PAD, GLA_KEY), F32)
    for d in range(2):
        wg = wg.at[:, d, d * GLA_RANK:(d + 1) * GLA_RANK].set(gla_w_gate[:, d])
    s5 = jax.vmap(_s5_matrices)(s5_lam_re, s5_lam_im, s5_log_dt, s5_b_re, s5_b_im, s5_c_re, s5_c_im, s5_d)
    wg = wg.astype(BF16)
    w_glu = s5_w_glu.astype(BF16)
    w_o = w_out.astype(BF16)
    return [{
        "gain": norm_g[i].reshape(1, D_MODEL),
        "w_in": w_pad[i],
        "s5": tuple(m[i] for m in s5),
        "wg": wg[i],
        "bg": gla_b_gate[i].reshape(2, 1, GLA_KEY),
        "w_glu": w_glu[i],
        "b_glu": s5_b_glu[i].reshape(1, S5_WIDTH),
        "gnorm": gla_norm_g[i].reshape(1, GLA_DV),
        "w_out": w_o[i],
        "fin": final_norm.reshape(1, D_MODEL),
    } for i in range(DEPTH)]


def kernel(x, c, ctx, c_ctx, norm_g, w_mod, b_mod, w_in, s5_lam_re, s5_lam_im, s5_log_dt, s5_b_re, s5_b_im,
           s5_c_re, s5_c_im, s5_d, s5_w_glu, s5_b_glu, gla_w_gate, gla_b_gate, gla_norm_g, w_out, final_norm):
    bsz = x.shape[0]
    cond8 = jnp.concatenate([c, c_ctx[None], jnp.zeros((8 - bsz - 1, D_MODEL), F32)], axis=0)
    mods_all = _modulation(cond8, w_mod, b_mod)
    weights = _stack_weights(norm_g, w_in, s5_lam_re, s5_lam_im, s5_log_dt, s5_b_re, s5_b_im, s5_c_re, s5_c_im,
                             s5_d, s5_w_glu, s5_b_glu, gla_w_gate, gla_b_gate, gla_norm_g, w_out, final_norm)
    x_lat, x_ctx = x, ctx
    for i in range(DEPTH):
        mods = mods_all[i].reshape(8, 1, 3 * D_MODEL)
        last = i == DEPTH - 1
        x_lat, x_ctx = _layer(x_lat, x_ctx, mods, weights[i], need_ctx_out=not last, final=last)
    return x_lat
```

```python
import functools

import jax
import jax.numpy as jnp
from jax import lax
from jax.experimental import pallas as pl
from jax.experimental.pallas import tpu as pltpu

F32 = jnp.float32
BF16 = jnp.bfloat16

D_MODEL = 1024
DEPTH = 4
GRID_W = 64
EPS = 1e-6
LANES = 128
SUBLANES = 8
S5_WIDTH = 512
S5_GROUP = 16
S5_GROUPS = 32
S5_STATE = 64
S5_CHUNK = 16
S5_ROW = S5_CHUNK * S5_GROUP
S5_HALF = 2 * S5_STATE
S5_SLANES = 2 * S5_HALF
S5_SCAN_BLOCK = 32
S5_GROUP_BLOCK = 8
SLOTS = LANES // S5_GROUP
GLA_HEADS = 4
GLA_DV = 128
GLA_DK = 64
GLA_WIDTH = 512
GLA_KEY = 256
GLA_RANK = 16
GLA_NORMALIZER = 16.0
GLA_CHUNK = 64
GLA_COL = 128
LR_PAD = 128
IN_PAD = 2 * S5_WIDTH + 2 * GLA_KEY + 2 * GLA_WIDTH + LR_PAD

ROW_BLOCK = 512
GRID_ROWS_PER_BLOCK = ROW_BLOCK // GRID_W
CHUNK_PITCH = S5_CHUNK + SUBLANES
GRID_ROW_PITCH = GRID_W + SUBLANES
VMEM_LIMIT = 48 * 1024 * 1024


def _cparams(*sem):
    return pltpu.CompilerParams(dimension_semantics=sem, vmem_limit_bytes=VMEM_LIMIT)


def _mod_kernel(cond_ref, w_ref, b_ref, o_ref):
    cnd = cond_ref[...]
    a = cnd * jax.nn.sigmoid(cnd)
    o_ref[...] = jnp.dot(a, w_ref[...], preferred_element_type=F32,
                         precision=lax.Precision.HIGHEST) + b_ref[...]


def _modulation(cond8, w_mod, b_mod):
    nblk = 3
    return pl.pallas_call(
        _mod_kernel,
        grid=(DEPTH, nblk),
        in_specs=[
            pl.BlockSpec((8, D_MODEL), lambda l, j: (0, 0)),
            pl.BlockSpec((None, D_MODEL, D_MODEL), lambda l, j: (l, 0, j)),
            pl.BlockSpec((None, 1, D_MODEL), lambda l, j: (l, 0, j)),
        ],
        out_specs=pl.BlockSpec((None, 8, D_MODEL), lambda l, j: (l, 0, j)),
        out_shape=jax.ShapeDtypeStruct((DEPTH, 8, 3 * D_MODEL), F32),
        compiler_params=_cparams("arbitrary", "arbitrary"),
        name="modulation",
    )(cond8, w_mod, b_mod.reshape(DEPTH, 1, 3 * D_MODEL))


_IN_SEGS = (("zs", S5_WIDTH, False), ("q", GLA_KEY, True), ("k", GLA_KEY, True), ("v", GLA_WIDTH, True),
            ("zg", GLA_WIDTH, False), ("lr", LR_PAD, True))
_BF16_SEGS = ("zs", "zg")


def _swap_slots_with_index(vs):
    lane = lax.broadcasted_iota(jnp.int32, vs[0].shape, 1)
    for d in (4, 2, 1):
        sh = d * S5_GROUP
        low = (lane & sh) == 0
        nxt = list(vs)
        for i in range(SLOTS):
            if i & d == 0:
                a, b = vs[i], vs[i + d]
                nxt[i] = jnp.where(low, a, pltpu.roll(b, sh, 1))
                nxt[i + d] = jnp.where(low, pltpu.roll(a, LANES - sh, 1), b)
        vs = nxt
    return vs


def _inproj_kernel(x_ref, mod_ref, gain_ref, w_ref, xg_ref, *rest, colmajor):
    out_refs, u_scr, c_scr = rest[:-2], rest[-2], rest[-1]
    x = x_ref[...]
    ms = jnp.mean(x * x, axis=-1, keepdims=True)
    y = x * lax.rsqrt(ms + EPS) * gain_ref[...]
    shift = mod_ref[:, 0:D_MODEL]
    scale = mod_ref[:, D_MODEL:2 * D_MODEL]
    h = (y * (1.0 + scale) + shift).astype(BF16)
    u = jnp.dot(h, w_ref[:, 0:S5_WIDTH], preferred_element_type=F32)
    nchunk = u.shape[0] // S5_CHUNK
    for kt in range(S5_WIDTH // LANES):
        for c in range(nchunk):
            u_scr[kt, c * CHUNK_PITCH:c * CHUNK_PITCH + S5_CHUNK, :] = u[c * S5_CHUNK:(c + 1) * S5_CHUNK,
                                                                         kt * LANES:(kt + 1) * LANES]
    off = S5_WIDTH
    for (_, width, cm), o_ref in zip(_IN_SEGS, out_refs):
        val = jnp.dot(h, w_ref[:, off:off + width], preferred_element_type=F32)
        off += width
        if colmajor and cm:
            for kt in range(width // LANES):
                for r in range(GRID_ROWS_PER_BLOCK):
                    c_scr[kt, r * GRID_ROW_PITCH:r * GRID_ROW_PITCH + GRID_W, :] = val[r * GRID_W:(r + 1) * GRID_W,
                                                                                       kt * LANES:(kt + 1) * LANES]
            for w in range(GRID_W):
                for kt in range(width // LANES):
                    o_ref[w, :, kt * LANES:(kt + 1) * LANES] = c_scr[kt, pl.ds(w, GRID_ROWS_PER_BLOCK,
                                                                              stride=GRID_ROW_PITCH), :]
        else:
            o_ref[...] = val.astype(o_ref.dtype)
    for half in range(S5_CHUNK // SLOTS):
        for kt in range(S5_WIDTH // LANES):
            vs = [u_scr[kt, pl.ds(half * SLOTS + jj, nchunk, stride=CHUNK_PITCH), :] for jj in range(SLOTS)]
            ws = _swap_slots_with_index(vs)
            for qq in range(SLOTS):
                xg_ref[kt * SLOTS + qq, :, half * LANES:(half + 1) * LANES] = ws[qq].astype(BF16)


def _inproj(x2, mods, gain, w_pad, mod_row, bsz, colmajor):
    rows = x2.shape[0]
    per_seq = rows // bsz
    tb = min(ROW_BLOCK, per_seq)
    bps = per_seq // tb
    out_specs = [pl.BlockSpec((None, S5_GROUPS, tb // S5_CHUNK, S5_ROW), lambda i: (i // bps, 0, i % bps, 0))]
    out_shape = [jax.ShapeDtypeStruct((bsz, S5_GROUPS, per_seq // S5_CHUNK, S5_ROW), BF16)]
    for name, w, cm in _IN_SEGS:
        if colmajor and cm:
            assert tb == ROW_BLOCK
            out_specs.append(pl.BlockSpec((None, GRID_W, GRID_ROWS_PER_BLOCK, w),
                                          lambda i: (i // bps, 0, i % bps, 0)))
            out_shape.append(jax.ShapeDtypeStruct((bsz, GRID_W, per_seq // GRID_W, w), F32))
        else:
            out_specs.append(pl.BlockSpec((tb, w), lambda i: (i, 0)))
            out_shape.append(jax.ShapeDtypeStruct((rows, w), BF16 if name in _BF16_SEGS else F32))
    return pl.pallas_call(
        functools.partial(_inproj_kernel, colmajor=colmajor),
        grid=(rows // tb,),
        in_specs=[
            pl.BlockSpec((tb, D_MODEL), lambda i: (i, 0)),
            pl.BlockSpec((None, 1, 3 * D_MODEL), lambda i: (mod_row(i), 0, 0)),
            pl.BlockSpec((1, D_MODEL), lambda i: (0, 0)),
            pl.BlockSpec((D_MODEL, IN_PAD), lambda i: (0, 0)),
        ],
        out_specs=out_specs,
        out_shape=out_shape,
        scratch_shapes=[pltpu.VMEM((S5_WIDTH // LANES, tb // S5_CHUNK * CHUNK_PITCH, LANES), F32),
                        pltpu.VMEM((GLA_WIDTH // LANES, GRID_ROWS_PER_BLOCK * GRID_ROW_PITCH, LANES), F32)],
        compiler_params=_cparams("arbitrary"),
        name="inproj",
    )(x2, mods, gain, w_pad)


def _s5_matrices(lam_re, lam_im, log_dt, b_re, b_im, c_re, c_im, d_skip):
    hp = lax.Precision.HIGHEST
    T = S5_CHUNK
    lam = lax.complex(lam_re.astype(F32), lam_im.astype(F32))
    dt = jnp.exp(log_dt.astype(F32))[..., None]
    ldt = lam * dt
    lam_bar = jnp.exp(ldt)
    b_c = lax.complex(b_re.astype(F32), b_im.astype(F32))
    b_bar = ((lam_bar - 1.0) / lam)[..., None] * b_c[None]
    c_c = lax.complex(c_re.astype(F32), c_im.astype(F32))
    steps = jnp.arange(T + 1, dtype=F32)
    pw = jnp.exp(ldt[:, None] * steps[None, :, None, None].astype(jnp.complex64))

    kern = jnp.einsum('gpn,dtgn,dgnq->dgqtp', c_c, pw[:, :T], b_bar, precision=hp).real
    skip = jnp.eye(S5_GROUP, dtype=F32)[None] * d_skip.astype(F32).reshape(S5_GROUPS, 1, S5_GROUP)
    k_f = kern[0].at[:, :, 0, :].add(skip).reshape(S5_GROUPS, S5_GROUP, S5_ROW)
    k_b = kern[1][:, :, ::-1].reshape(S5_GROUPS, S5_GROUP, S5_ROW)
    zero_rows = jnp.zeros_like(k_f)
    toep_rows = jnp.stack([jnp.concatenate([zero_rows, k_f], axis=-1),
                           jnp.concatenate([k_b, zero_rows], axis=-1)], axis=1)

    lanes = lambda parts: jnp.concatenate(parts, axis=-1)
    by_group = lambda p: p.transpose(1, 0, 2)

    qf = by_group(pw[0][:T][::-1])
    qb = by_group(pw[1][:T])
    bt = b_bar.transpose(0, 1, 3, 2)
    inj_rows = jnp.stack([lanes([qf.real, qf.real, qb.real, qb.real]),
                          lanes([-qf.imag, qf.imag, -qb.imag, qb.imag]),
                          lanes([bt[0].real, bt[0].imag, bt[1].real, bt[1].imag]),
                          lanes([bt[0].imag, bt[0].real, bt[1].imag, bt[1].real])], axis=1)

    pf = by_group(pw[0][1:T + 1])
    pb = by_group(pw[1][::-1][:T])
    out_rows = jnp.stack([lanes([pf.real, -pf.imag, pb.real, -pb.imag]),
                          lanes([-pf.imag, -pf.real, -pb.imag, -pb.real]),
                          lanes([c_c.real] * 4),
                          lanes([c_c.imag] * 4)], axis=1)

    a_chunk = pw[:, T]
    a1 = jnp.concatenate([a_chunk.real, a_chunk.real], axis=-1)
    a2 = jnp.concatenate([-a_chunk.imag, a_chunk.imag], axis=-1)
    return toep_rows, out_rows, inj_rows, a1, a2


def _factor_rows_to_matrix(r_ref, gi, dst):
    s_a = r_ref[gi, 2]
    s_b = r_ref[gi, 3]
    for a in range(S5_CHUNK):
        rows = r_ref[gi, 0, a:a + 1, :] * s_a + r_ref[gi, 1, a:a + 1, :] * s_b
        dst[gi, a * S5_GROUP:(a + 1) * S5_GROUP, :] = rows.astype(BF16)


def _s5_inject_kernel(x_ref, r_ref, sf_ref, sb_ref, m_scr):
    for gi in range(S5_GROUP_BLOCK):
        _factor_rows_to_matrix(r_ref, gi, m_scr)
        s = jnp.dot(x_ref[gi], m_scr[gi], preferred_element_type=F32)
        sf_ref[gi] = s[:, 0:S5_HALF]
        sb_ref[gi] = s[:, S5_HALF:S5_SLANES]


def _s5_inject(xg, inj_rows):
    bsz, _, nchunk, _ = xg.shape
    gb = S5_GROUP_BLOCK
    s_spec = pl.BlockSpec((None, gb, nchunk, S5_HALF), lambda b, g: (b, g, 0, 0))
    s_shape = jax.ShapeDtypeStruct((bsz, S5_GROUPS, nchunk, S5_HALF), F32)
    return pl.pallas_call(
        _s5_inject_kernel,
        grid=(bsz, S5_GROUPS // gb),
        in_specs=[
            pl.BlockSpec((None, gb, nchunk, S5_ROW), lambda b, g: (b, g, 0, 0)),
            pl.BlockSpec((gb, 4, S5_GROUP, S5_SLANES), lambda b, g: (g, 0, 0, 0)),
        ],
        out_specs=[s_spec, s_spec],
        out_shape=[s_shape, s_shape],
        scratch_shapes=[pltpu.VMEM((gb, S5_ROW, S5_SLANES), BF16)],
        compiler_params=_cparams("arbitrary", "arbitrary"),
        name="s5_inject",
    )(xg, inj_rows)


def _s5_scan_kernel(sf_ref, sb_ref, a1_ref, a2_ref, h0_ref, hf_ref, hb_ref, hfin_ref, hf_scr, hb_scr, *, cpb):
    @pl.when(pl.program_id(0) == 0)
    def _():
        hf_scr[...] = h0_ref[0]
        hb_scr[...] = h0_ref[1]

    a1f = a1_ref[0]
    a2f = a2_ref[0]
    a1b = a1_ref[1]
    a2b = a2_ref[1]
    hf = hf_scr[...]
    hb = hb_scr[...]
    for c in range(cpb):
        hf_ref[:, c * S5_HALF:(c + 1) * S5_HALF] = hf.astype(BF16)
        hf = a1f * hf + a2f * pltpu.roll(hf, S5_STATE, 1) + sf_ref[:, c * S5_HALF:(c + 1) * S5_HALF]
        cb = cpb - 1 - c
        hb_ref[:, cb * S5_HALF:(cb + 1) * S5_HALF] = hb.astype(BF16)
        hb = a1b * hb + a2b * pltpu.roll(hb, S5_STATE, 1) + sb_ref[:, cb * S5_HALF:(cb + 1) * S5_HALF]
    hf_scr[...] = hf
    hb_scr[...] = hb

    @pl.when(pl.program_id(0) == pl.num_programs(0) - 1)
    def _():
        hfin_ref[0] = hf
        hfin_ref[1] = hb


def _s5_scan(s_f, s_b, a1, a2, h0):
    bsz, _, nchunk, _ = s_f.shape
    rows = bsz * S5_GROUPS
    cpb = min(S5_SCAN_BLOCK, nchunk)
    nblk = nchunk // cpb
    fwd = pl.BlockSpec((rows, cpb * S5_HALF), lambda i: (0, i))
    bwd = pl.BlockSpec((rows, cpb * S5_HALF), lambda i: (0, nblk - 1 - i))
    small = pl.BlockSpec((2, rows, S5_HALF), lambda i: (0, 0, 0))
    hf, hb, hfin = pl.pallas_call(
        functools.partial(_s5_scan_kernel, cpb=cpb),
        grid=(nblk,),
        in_specs=[fwd, bwd, small, small, small],
        out_specs=[fwd, bwd, small],
        out_shape=[jax.ShapeDtypeStruct((rows, nchunk * S5_HALF), BF16)] * 2
        + [jax.ShapeDtypeStruct((2, rows, S5_HALF), F32)],
        scratch_shapes=[pltpu.VMEM((rows, S5_HALF), F32), pltpu.VMEM((rows, S5_HALF), F32)],
        compiler_params=_cparams("arbitrary"),
        name="s5_scan",
    )(s_f.reshape(rows, nchunk * S5_HALF), s_b.reshape(rows, nchunk * S5_HALF), a1, a2, h0)
    shape4 = (bsz, S5_GROUPS, nchunk, S5_HALF)
    return hf.reshape(shape4), hb.reshape(shape4), hfin


def _dot_nt(a, b):
    return lax.dot_general(a, b, (((1,), (1,)), ((), ())), preferred_element_type=F32)


def _s5_output_kernel(x_ref, hf_ref, hb_ref, t_ref, r_ref, y_ref, m_scr, c_scr):
    for gi in range(S5_GROUP_BLOCK):
        k_f = t_ref[gi, 0]
        k_b = t_ref[gi, 1]
        for i in range(S5_CHUNK):
            lo_f = S5_ROW - i * S5_GROUP
            lo_b = (S5_CHUNK - 1 - i) * S5_GROUP
            m_scr[gi, i * S5_GROUP:(i + 1) * S5_GROUP, :] = (k_f[:, lo_f:lo_f + S5_ROW]
                                                             + k_b[:, lo_b:lo_b + S5_ROW]).astype(BF16)
        _factor_rows_to_matrix(r_ref, gi, c_scr)
        y = jnp.dot(x_ref[gi], m_scr[gi], preferred_element_type=F32)
        y = y + _dot_nt(hf_ref[gi], c_scr[gi, :, 0:S5_HALF])
        y = y + _dot_nt(hb_ref[gi], c_scr[gi, :, S5_HALF:S5_SLANES])
        y_ref[gi] = y.astype(BF16)


def _s5_output(xg, hf, hb, toep_rows, out_rows):
    bsz, _, nchunk, _ = xg.shape
    gb = S5_GROUP_BLOCK
    return pl.pallas_call(
        _s5_output_kernel,
        grid=(bsz, S5_GROUPS // gb),
        in_specs=[
            pl.BlockSpec((None, gb, nchunk, S5_ROW), lambda b, g: (b, g, 0, 0)),
            pl.BlockSpec((None, gb, nchunk, S5_HALF), lambda b, g: (b, g, 0, 0)),
            pl.BlockSpec((None, gb, nchunk, S5_HALF), lambda b, g: (b, g, 0, 0)),
            pl.BlockSpec((gb, 2, S5_GROUP, 2 * S5_ROW), lambda b, g: (g, 0, 0, 0)),
            pl.BlockSpec((gb, 4, S5_GROUP, S5_SLANES), lambda b, g: (g, 0, 0, 0)),
        ],
        out_specs=pl.BlockSpec((None, gb, nchunk, S5_ROW), lambda b, g: (b, g, 0, 0)),
        out_shape=jax.ShapeDtypeStruct((bsz, S5_GROUPS, nchunk, S5_ROW), BF16),
        scratch_shapes=[pltpu.VMEM((gb, S5_ROW, S5_ROW), BF16), pltpu.VMEM((gb, S5_ROW, S5_SLANES), BF16)],
        compiler_params=_cparams("arbitrary", "arbitrary"),
        name="s5_output",
    )(xg, hf, hb, toep_rows, out_rows)


def _s5_branch(xg_c, xg_l, mats, need_ctx):
    toep_rows, out_rows, inj_rows, a1, a2 = mats
    bsz = xg_l.shape[0]
    a1r = jnp.tile(a1, (1, bsz, 1))
    a2r = jnp.tile(a2, (1, bsz, 1))
    sf_c, sb_c = _s5_inject(xg_c, inj_rows)
    sf_l, sb_l = _s5_inject(xg_l, inj_rows)
    hf_c, hb_c, h_ctx = _s5_scan(sf_c, sb_c, a1r, a2r, jnp.zeros_like(a1r))
    hf_l, hb_l, _ = _s5_scan(sf_l, sb_l, a1r, a2r, h_ctx)
    y_l = _s5_output(xg_l, hf_l, hb_l, toep_rows, out_rows)
    y_c = _s5_output(xg_c, hf_c, hb_c, toep_rows, out_rows) if need_ctx else None
    return y_c, y_l


def _gla_kernel(qf_ref, kf_ref, vf_ref, lrf_ref, qb_ref, kb_ref, vb_ref, lrb_ref, wg_ref, bg_ref, s0_ref,
                of_ref, ob_ref, sfin_ref, s_scr):
    w = pl.program_id(0)

    @pl.when(w == 0)
    def _():
        s_scr[...] = s0_ref[...]

    row = lax.broadcasted_iota(jnp.int32, (GLA_COL, GLA_COL), 0)
    col = lax.broadcasted_iota(jnp.int32, (GLA_COL, GLA_COL), 1)
    same_chunk = (row // GLA_CHUNK) == (col // GLA_CHUNK)
    r64 = lax.broadcasted_iota(jnp.int32, (GLA_CHUNK, GLA_CHUNK), 0)
    c64 = lax.broadcasted_iota(jnp.int32, (GLA_CHUNK, GLA_CHUNK), 1)
    dirs = (
        (qf_ref, kf_ref, vf_ref, lrf_ref, of_ref, jnp.where(same_chunk & (col <= row), 1.0, 0.0).astype(BF16),
         c64 <= r64),
        (qb_ref, kb_ref, vb_ref, lrb_ref, ob_ref, jnp.where(same_chunk & (col >= row), 1.0, 0.0).astype(BF16),
         c64 >= r64),
    )
    chains = [(b, d) for b in range(s_scr.shape[0]) for d in range(2)]

    pre = []
    for b, d in chains:
        q_ref, k_ref, _, lr_ref, _, tri, _ = dirs[d]
        z = jnp.dot(lr_ref[b].astype(BF16), wg_ref[d], preferred_element_type=F32) + bg_ref[d]
        g = (jnp.minimum(z, 0.0) - jnp.log1p(jnp.exp(-jnp.abs(z)))) * (1.0 / GLA_NORMALIZER)
        g_hi = g.astype(BF16)
        g_lo = (g - g_hi.astype(F32)).astype(BF16)
        gc = jnp.dot(tri, g_hi, preferred_element_type=F32) + jnp.dot(tri, g_lo, preferred_element_type=F32)
        qt = (q_ref[b] * jnp.exp(gc) * (GLA_DK ** -0.5)).astype(BF16)
        pre.append((qt, gc.T, k_ref[b].T))

    for pos in range(2):
        keys = []
        for (b, d), (qt, gct, kt) in zip(chains, pre):
            ci = pos if d == 0 else 1 - pos
            r0 = ci * GLA_CHUNK
            last = r0 + GLA_CHUNK - 1 if d == 0 else r0
            gct_c = gct[:, r0:r0 + GLA_CHUNK]
            dec = jnp.exp(gct[:, last:last + 1])
            k_in = kt[:, r0:r0 + GLA_CHUNK] * jnp.exp(-gct_c)
            keys.append((r0, dec, k_in.astype(BF16), (k_in * dec).astype(BF16)))
        for h in range(GLA_HEADS):
            ks = slice(h * GLA_DK, (h + 1) * GLA_DK)
            vs = slice(h * GLA_DV, (h + 1) * GLA_DV)
            scores = []
            for (b, d), (qt, _, _), (r0, _, k_in, _) in zip(chains, pre, keys):
                sc = jnp.dot(qt[r0:r0 + GLA_CHUNK, ks], k_in[ks, :], preferred_element_type=F32)
                scores.append(jnp.where(dirs[d][6], sc, 0.0).astype(BF16))
            for (b, d), (qt, _, _), (r0, dec, _, k_st), sc in zip(chains, pre, keys, scores):
                v_h = dirs[d][2][b, r0:r0 + GLA_CHUNK, vs].astype(BF16)
                s_h = s_scr[b, d, ks, :]
                o_h = jnp.dot(sc, v_h, preferred_element_type=F32)
                o_h = o_h + jnp.dot(qt[r0:r0 + GLA_CHUNK, ks], s_h.astype(BF16), preferred_element_type=F32)
                dirs[d][4][b, r0:r0 + GLA_CHUNK, vs] = o_h
                s_scr[b, d, ks, :] = dec[ks, :] * s_h + jnp.dot(k_st[ks, :], v_h, preferred_element_type=F32)

    @pl.when(w == pl.num_programs(0) - 1)
    def _():
        sfin_ref[...] = s_scr[...]


def _gla(q, k, v, lr, wg, bg, s0):
    bsz, ncol = q.shape[:2]
    fwd = lambda w: (0, w, 0, 0)
    bwd = lambda w: (0, ncol - 1 - w, 0, 0)
    widths = (GLA_KEY, GLA_KEY, GLA_WIDTH, LR_PAD)
    state_spec = pl.BlockSpec((bsz, 2, GLA_KEY, GLA_DV), lambda w: (0, 0, 0, 0))
    in_specs = ([pl.BlockSpec((bsz, None, GLA_COL, wd), fwd) for wd in widths]
                + [pl.BlockSpec((bsz, None, GLA_COL, wd), bwd) for wd in widths]
                + [pl.BlockSpec((2, LR_PAD, GLA_KEY), lambda w: (0, 0, 0)),
                   pl.BlockSpec((2, 1, GLA_KEY), lambda w: (0, 0, 0)),
                   state_spec])
    o_shape = jax.ShapeDtypeStruct((bsz, ncol, GLA_COL, GLA_WIDTH), F32)
    return pl.pallas_call(
        _gla_kernel,
        grid=(ncol,),
        in_specs=in_specs,
        out_specs=[pl.BlockSpec((bsz, None, GLA_COL, GLA_WIDTH), fwd),
                   pl.BlockSpec((bsz, None, GLA_COL, GLA_WIDTH), bwd),
                   state_spec],
        out_shape=[o_shape, o_shape, jax.ShapeDtypeStruct((bsz, 2, GLA_KEY, GLA_DV), F32)],
        scratch_shapes=[pltpu.VMEM((bsz, 2, GLA_KEY, GLA_DV), F32)],
        compiler_params=_cparams("arbitrary"),
        name="gla",
    )(q, k, v, lr, q, k, v, lr, wg, bg, s0)


def _finish_kernel(ys_ref, zs_ref, of_ref, ob_ref, zg_ref, x_ref, mod_ref, wglu_ref, bglu_ref, gn_ref, wout_ref,
                   fin_ref, o_ref, ys_scr, o_scr, *, final, colmajor):
    nchunk = ys_ref.shape[1]
    for half in range(S5_CHUNK // SLOTS):
        for kt in range(S5_WIDTH // LANES):
            ws = [ys_ref[kt * SLOTS + qq, :, half * LANES:(half + 1) * LANES].astype(F32) for qq in range(SLOTS)]
            vs = _swap_slots_with_index(ws)
            for jj in range(SLOTS):
                ys_scr[kt, pl.ds(half * SLOTS + jj, nchunk, stride=CHUNK_PITCH), :] = vs[jj]
    ys = jnp.concatenate(
        [jnp.concatenate([ys_scr[kt, c * CHUNK_PITCH:c * CHUNK_PITCH + S5_CHUNK, :] for c in range(nchunk)], axis=0)
         for kt in range(S5_WIDTH // LANES)], axis=1)
    s = jax.nn.gelu(ys, approximate=True)
    t = jnp.dot(s.astype(BF16), wglu_ref[...], preferred_element_type=F32) + bglu_ref[...]
    zs = zs_ref[...].astype(F32)
    a_s = (s * jax.nn.sigmoid(t)) * (zs * jax.nn.sigmoid(zs))
    y = jnp.dot(a_s.astype(BF16), wout_ref[0:S5_WIDTH, :], preferred_element_type=F32)

    if colmajor:
        for w in range(GRID_W):
            o_w = of_ref[w] + ob_ref[w]
            for h in range(GLA_HEADS):
                o_scr[h, pl.ds(w, GRID_ROWS_PER_BLOCK, stride=GRID_ROW_PITCH), :] = o_w[:, h * GLA_DV:(h + 1) * GLA_DV]
        o_heads = [jnp.concatenate([o_scr[h, r * GRID_ROW_PITCH:r * GRID_ROW_PITCH + GRID_W, :]
                                    for r in range(GRID_ROWS_PER_BLOCK)], axis=0) for h in range(GLA_HEADS)]
    else:
        o_sum = of_ref[...] + ob_ref[...]
        o_heads = [o_sum[:, h * GLA_DV:(h + 1) * GLA_DV] for h in range(GLA_HEADS)]
    zg = zg_ref[...].astype(F32)
    gate_g = zg * jax.nn.sigmoid(zg)
    for h in range(GLA_HEADS):
        o_h = o_heads[h]
        o_n = o_h * lax.rsqrt(jnp.mean(o_h * o_h, axis=-1, keepdims=True) + EPS) * gn_ref[...]
        a_h = (o_n * gate_g[:, h * GLA_DV:(h + 1) * GLA_DV]).astype(BF16)
        y = y + jnp.dot(a_h, wout_ref[S5_WIDTH + h * GLA_DV:S5_WIDTH + (h + 1) * GLA_DV, :],
                        preferred_element_type=F32)
    gate = mod_ref[:, 2 * D_MODEL:3 * D_MODEL]
    xn = x_ref[...] + gate * y
    if final:
        ms = jnp.mean(xn * xn, axis=-1, keepdims=True)
        xn = xn * lax.rsqrt(ms + EPS) * fin_ref[...]
    o_ref[...] = xn


def _finish(ys, zs, o_f, o_b, zg, x2, mods, mod_row, w_glu, b_glu, gnorm, w_out, fin_gain, final, colmajor):
    rows = x2.shape[0]
    per_seq = rows // ys.shape[0]
    tb = min(ROW_BLOCK, per_seq)
    bps = per_seq // tb
    rb = lambda w: pl.BlockSpec((tb, w), lambda i: (i, 0))
    full = lambda a: pl.BlockSpec(a.shape, lambda i: (0,) * a.ndim)
    if colmajor:
        assert tb == ROW_BLOCK
        o_spec = pl.BlockSpec((None, GRID_W, GRID_ROWS_PER_BLOCK, GLA_WIDTH), lambda i: (i // bps, 0, i % bps, 0))
    else:
        o_spec = rb(GLA_WIDTH)
    return pl.pallas_call(
        functools.partial(_finish_kernel, final=final, colmajor=colmajor),
        grid=(rows // tb,),
        in_specs=[pl.BlockSpec((None, S5_GROUPS, tb // S5_CHUNK, S5_ROW), lambda i: (i // bps, 0, i % bps, 0)),
                  rb(S5_WIDTH), o_spec, o_spec, rb(GLA_WIDTH), rb(D_MODEL),
                  pl.BlockSpec((None, 1, 3 * D_MODEL), lambda i: (mod_row(i), 0, 0)),
                  full(w_glu), full(b_glu), full(gnorm), full(w_out), full(fin_gain)],
        out_specs=rb(D_MODEL),
        out_shape=jax.ShapeDtypeStruct((rows, D_MODEL), F32),
        scratch_shapes=[pltpu.VMEM((S5_WIDTH // LANES, tb // S5_CHUNK * CHUNK_PITCH, LANES), F32),
                        pltpu.VMEM((GLA_HEADS, GRID_ROWS_PER_BLOCK * GRID_ROW_PITCH, GLA_DV), F32)],
        compiler_params=_cparams("arbitrary"),
        name="finish",
    )(ys, zs, o_f, o_b, zg, x2, mods, w_glu, b_glu, gnorm, w_out, fin_gain)


def _layer(x_lat, x_ctx, mods, lw, need_ctx_out, final):
    bsz, l, _ = x_lat.shape
    lc = x_ctx.shape[1]
    blocks_per_batch = l // ROW_BLOCK
    lat_row = lambda i: i // blocks_per_batch
    ctx_row = lambda i: bsz

    pj_l = _inproj(x_lat.reshape(bsz * l, D_MODEL), mods, lw["gain"], lw["w_in"], lat_row, bsz, colmajor=True)
    pj_c = _inproj(x_ctx.reshape(bsz * lc, D_MODEL), mods, lw["gain"], lw["w_in"], ctx_row, bsz, colmajor=False)
    xg_l, zs_l, q_l, k_l, v_l, zg_l, lr_l = pj_l
    xg_c, zs_c, q_c, k_c, v_c, zg_c, lr_c = pj_c

    ys_c, ys_l = _s5_branch(xg_c, xg_l, lw["s5"], need_ctx_out)

    cols = lambda t: t.reshape(bsz, lc // GLA_COL, GLA_COL, t.shape[-1])
    s_zero = jnp.zeros((bsz, 2, GLA_KEY, GLA_DV), F32)
    ocf, ocb, s_ctx = _gla(cols(q_c), cols(k_c), cols(v_c), cols(lr_c), lw["wg"], lw["bg"], s_zero)
    olf, olb, _ = _gla(q_l, k_l, v_l, lr_l, lw["wg"], lw["bg"], s_ctx)

    fin_args = (lw["w_glu"], lw["b_glu"], lw["gnorm"], lw["w_out"], lw["fin"])
    x_lat_new = _finish(ys_l, zs_l, olf, olb, zg_l, x_lat.reshape(bsz * l, D_MODEL), mods, lat_row,
                        *fin_args, final=final, colmajor=True).reshape(bsz, l, D_MODEL)
    x_ctx_new = x_ctx
    if need_ctx_out:
        x_ctx_new = _finish(ys_c, zs_c, ocf.reshape(bsz * lc, GLA_WIDTH), ocb.reshape(bsz * lc, GLA_WIDTH), zg_c,
                            x_ctx.reshape(bsz * lc, D_MODEL), mods, ctx_row, *fin_args, final=False,
                            colmajor=False).reshape(bsz, lc, D_MODEL)
    return x_lat_new, x_ctx_new


def _stack_weights(norm_g, w_in, s5_lam_re, s5_lam_im, s5_log_dt, s5_b_re, s5_b_im, s5_c_re, s5_c_im, s5_d,
                   s5_w_glu, s5_b_glu, gla_w_gate, gla_b_gate, gla_norm_g, w_out, final_norm):
    w_pad = jnp.pad(w_in, ((0, 0), (0, 0), (0, IN_PAD - w_in.shape[-1]))).astype(BF16)
    wg = jnp.zeros((DEPTH, 2, LR_PAD, GLA_KEY), F32)
    for d in range(2):
        wg = wg.at[:, d, d * GLA_RANK:(d + 1) * GLA_RANK].set(gla_w_gate[:, d])
    s5 = jax.vmap(_s5_matrices)(s5_lam_re, s5_lam_im, s5_log_dt, s5_b_re, s5_b_im, s5_c_re, s5_c_im, s5_d)
    wg = wg.astype(BF16)
    w_glu = s5_w_glu.astype(BF16)
    w_o = w_out.astype(BF16)
    return [{
        "gain": norm_g[i].reshape(1, D_MODEL),
        "w_in": w_pad[i],
        "s5": tuple(m[i] for m in s5),
        "wg": wg[i],
        "bg": gla_b_gate[i].reshape(2, 1, GLA_KEY),
        "w_glu": w_glu[i],
        "b_glu": s5_b_glu[i].reshape(1, S5_WIDTH),
        "gnorm": gla_norm_g[i].reshape(1, GLA_DV),
        "w_out": w_o[i],
        "fin": final_norm.reshape(1, D_MODEL),
    } for i in range(DEPTH)]


def kernel(x, c, ctx, c_ctx, norm_g, w_mod, b_mod, w_in, s5_lam_re, s5_lam_im, s5_log_dt, s5_b_re, s5_b_im,
           s5_c_re, s5_c_im, s5_d, s5_w_glu, s5_b_glu, gla_w_gate, gla_b_gate, gla_norm_g, w_out, final_norm):
    bsz = x.shape[0]
    cond8 = jnp.concatenate([c, c_ctx[None], jnp.zeros((8 - bsz - 1, D_MODEL), F32)], axis=0)
    mods_all = _modulation(cond8, w_mod, b_mod)
    weights = _stack_weights(norm_g, w_in, s5_lam_re, s5_lam_im, s5_log_dt, s5_b_re, s5_b_im, s5_c_re, s5_c_im,
                             s5_d, s5_w_glu, s5_b_glu, gla_w_gate, gla_b_gate, gla_norm_g, w_out, final_norm)
    x_lat, x_ctx = x, ctx
    for i in range(DEPTH):
        mods = mods_all[i].reshape(8, 1, 3 * D_MODEL)
        last = i == DEPTH - 1
        x_lat, x_ctx = _layer(x_lat, x_ctx, mods, weights[i], need_ctx_out=not last, final=last)
    return x_lat
```

```python
import functools

import jax
import jax.numpy as jnp
from jax import lax
from jax.experimental import pallas as pl
from jax.experimental.pallas import tpu as pltpu

F32 = jnp.float32
BF16 = jnp.bfloat16

D_MODEL = 1024
DEPTH = 4
GRID_W = 64
EPS = 1e-6
LANES = 128
SUBLANES = 8
S5_WIDTH = 512
S5_GROUP = 16
S5_GROUPS = 32
S5_STATE = 64
S5_CHUNK = 16
S5_ROW = S5_CHUNK * S5_GROUP
S5_HALF = 2 * S5_STATE
S5_SLANES = 2 * S5_HALF
S5_SCAN_BLOCK = 32
S5_GROUP_BLOCK = 8
SLOTS = LANES // S5_GROUP
GLA_HEADS = 4
GLA_DV = 128
GLA_DK = 64
GLA_WIDTH = 512
GLA_KEY = 256
GLA_RANK = 16
GLA_NORMALIZER = 16.0
GLA_CHUNK = 64
GLA_COL = 128
LR_PAD = 128
IN_PAD = 2 * S5_WIDTH + 2 * GLA_KEY + 2 * GLA_WIDTH + LR_PAD

ROW_BLOCK = 512
GRID_ROWS_PER_BLOCK = ROW_BLOCK // GRID_W
CHUNK_PITCH = S5_CHUNK + SUBLANES
GRID_ROW_PITCH = GRID_W + SUBLANES
VMEM_LIMIT = 48 * 1024 * 1024


def _cparams(*sem):
    return pltpu.CompilerParams(dimension_semantics=sem, vmem_limit_bytes=VMEM_LIMIT)


def _mod_kernel(cond_ref, w_ref, b_ref, o_ref):
    cnd = cond_ref[...]
    a = cnd * jax.nn.sigmoid(cnd)
    o_ref[...] = jnp.dot(a, w_ref[...], preferred_element_type=F32,
                         precision=lax.Precision.HIGHEST) + b_ref[...]


def _modulation(cond8, w_mod, b_mod):
    nblk = 3
    return pl.pallas_call(
        _mod_kernel,
        grid=(DEPTH, nblk),
        in_specs=[
            pl.BlockSpec((8, D_MODEL), lambda l, j: (0, 0)),
            pl.BlockSpec((None, D_MODEL, D_MODEL), lambda l, j: (l, 0, j)),
            pl.BlockSpec((None, 1, D_MODEL), lambda l, j: (l, 0, j)),
        ],
        out_specs=pl.BlockSpec((None, 8, D_MODEL), lambda l, j: (l, 0, j)),
        out_shape=jax.ShapeDtypeStruct((DEPTH, 8, 3 * D_MODEL), F32),
        compiler_params=_cparams("arbitrary", "arbitrary"),
        name="modulation",
    )(cond8, w_mod, b_mod.reshape(DEPTH, 1, 3 * D_MODEL))


_IN_SEGS = (("zs", S5_WIDTH, False), ("q", GLA_KEY, True), ("k", GLA_KEY, True), ("v", GLA_WIDTH, True),
            ("zg", GLA_WIDTH, False), ("lr", LR_PAD, True))
_BF16_SEGS = ("zs", "zg")


def _swap_slots_with_index(vs):
    lane = lax.broadcasted_iota(jnp.int32, vs[0].shape, 1)
    for d in (4, 2, 1):
        sh = d * S5_GROUP
        low = (lane & sh) == 0
        nxt = list(vs)
        for i in range(SLOTS):
            if i & d == 0:
                a, b = vs[i], vs[i + d]
                nxt[i] = jnp.where(low, a, pltpu.roll(b, sh, 1))
                nxt[i + d] = jnp.where(low, pltpu.roll(a, LANES - sh, 1), b)
        vs = nxt
    return vs


def _inproj_kernel(x_ref, mod_ref, gain_ref, w_ref, xg_ref, *rest, colmajor):
    out_refs, u_scr, c_scr = rest[:-2], rest[-2], rest[-1]
    x = x_ref[...]
    ms = jnp.mean(x * x, axis=-1, keepdims=True)
    y = x * lax.rsqrt(ms + EPS) * gain_ref[...]
    shift = mod_ref[:, 0:D_MODEL]
    scale = mod_ref[:, D_MODEL:2 * D_MODEL]
    h = (y * (1.0 + scale) + shift).astype(BF16)
    u = jnp.dot(h, w_ref[:, 0:S5_WIDTH], preferred_element_type=F32)
    nchunk = u.shape[0] // S5_CHUNK
    for kt in range(S5_WIDTH // LANES):
        for c in range(nchunk):
            u_scr[kt, c * CHUNK_PITCH:c * CHUNK_PITCH + S5_CHUNK, :] = u[c * S5_CHUNK:(c + 1) * S5_CHUNK,
                                                                         kt * LANES:(kt + 1) * LANES]
    off = S5_WIDTH
    for (_, width, cm), o_ref in zip(_IN_SEGS, out_refs):
        val = jnp.dot(h, w_ref[:, off:off + width], preferred_element_type=F32)
        off += width
        if colmajor and cm:
            for kt in range(width // LANES):
                for r in range(GRID_ROWS_PER_BLOCK):
                    c_scr[kt, r * GRID_ROW_PITCH:r * GRID_ROW_PITCH + GRID_W, :] = val[r * GRID_W:(r + 1) * GRID_W,
                                                                                       kt * LANES:(kt + 1) * LANES]
            for w in range(GRID_W):
                for kt in range(width // LANES):
                    o_ref[w, :, kt * LANES:(kt + 1) * LANES] = c_scr[kt, pl.ds(w, GRID_ROWS_PER_BLOCK,
                                                                              stride=GRID_ROW_PITCH), :]
        else:
            o_ref[...] = val.astype(o_ref.dtype)
    for half in range(S5_CHUNK // SLOTS):
        for kt in range(S5_WIDTH // LANES):
            vs = [u_scr[kt, pl.ds(half * SLOTS + jj, nchunk, stride=CHUNK_PITCH), :] for jj in range(SLOTS)]
            ws = _swap_slots_with_index(vs)
            for qq in range(SLOTS):
                xg_ref[kt * SLOTS + qq, :, half * LANES:(half + 1) * LANES] = ws[qq].astype(BF16)


def _inproj(x2, mods, gain, w_pad, mod_row, bsz, colmajor):
    rows = x2.shape[0]
    per_seq = rows // bsz
    tb = min(ROW_BLOCK, per_seq)
    bps = per_seq // tb
    out_specs = [pl.BlockSpec((None, S5_GROUPS, tb // S5_CHUNK, S5_ROW), lambda i: (i // bps, 0, i % bps, 0))]
    out_shape = [jax.ShapeDtypeStruct((bsz, S5_GROUPS, per_seq // S5_CHUNK, S5_ROW), BF16)]
    for name, w, cm in _IN_SEGS:
        if colmajor and cm:
            assert tb == ROW_BLOCK
            out_specs.append(pl.BlockSpec((None, GRID_W, GRID_ROWS_PER_BLOCK, w),
                                          lambda i: (i // bps, 0, i % bps, 0)))
            out_shape.append(jax.ShapeDtypeStruct((bsz, GRID_W, per_seq // GRID_W, w), F32))
        else:
            out_specs.append(pl.BlockSpec((tb, w), lambda i: (i, 0)))
            out_shape.append(jax.ShapeDtypeStruct((rows, w), BF16 if name in _BF16_SEGS else F32))
    return pl.pallas_call(
        functools.partial(_inproj_kernel, colmajor=colmajor),
        grid=(rows // tb,),
        in_specs=[
            pl.BlockSpec((tb, D_MODEL), lambda i: (i, 0)),
            pl.BlockSpec((None, 1, 3 * D_MODEL), lambda i: (mod_row(i), 0, 0)),
            pl.BlockSpec((1, D_MODEL), lambda i: (0, 0)),
            pl.BlockSpec((D_MODEL, IN_PAD), lambda i: (0, 0)),
        ],
        out_specs=out_specs,
        out_shape=out_shape,
        scratch_shapes=[pltpu.VMEM((S5_WIDTH // LANES, tb // S5_CHUNK * CHUNK_PITCH, LANES), F32),
                        pltpu.VMEM((GLA_WIDTH // LANES, GRID_ROWS_PER_BLOCK * GRID_ROW_PITCH, LANES), F32)],
        compiler_params=_cparams("arbitrary"),
        name="inproj",
    )(x2, mods, gain, w_pad)


def _s5_matrices(lam_re, lam_im, log_dt, b_re, b_im, c_re, c_im, d_skip):
    T = S5_CHUNK
    lam = lax.complex(lam_re.astype(F32), lam_im.astype(F32))
    dt = jnp.exp(log_dt.astype(F32))[..., None]
    ldt = lam * dt
    lam_bar = jnp.exp(ldt)
    b_c = lax.complex(b_re.astype(F32), b_im.astype(F32))
    b_bar = ((lam_bar - 1.0) / lam)[..., None] * b_c[None]
    c_c = lax.complex(c_re.astype(F32), c_im.astype(F32))
    steps = jnp.arange(T + 1, dtype=F32)
    pw = jnp.exp(ldt[:, None] * steps[None, :, None, None].astype(jnp.complex64))

    lanes = lambda parts: jnp.concatenate(parts, axis=-1)
    by_group = lambda p: p.transpose(1, 0, 2)

    qf = by_group(pw[0][:T][::-1])
    qb = by_group(pw[1][:T])
    bt = b_bar.transpose(0, 1, 3, 2)
    inj_rows = jnp.stack([lanes([qf.real, qf.real, qb.real, qb.real]),
                          lanes([-qf.imag, qf.imag, -qb.imag, qb.imag]),
                          lanes([bt[0].real, bt[0].imag, bt[1].real, bt[1].imag]),
                          lanes([bt[0].imag, bt[0].real, bt[1].imag, bt[1].real])], axis=1)

    pf = by_group(pw[0][1:T + 1])
    pb = by_group(pw[1][::-1][:T])
    out_rows = jnp.stack([lanes([pf.real, -pf.imag, pb.real, -pb.imag]),
                          lanes([-pf.imag, -pf.real, -pb.imag, -pb.real]),
                          lanes([c_c.real] * 4),
                          lanes([c_c.imag] * 4)], axis=1)

    kf_pw = by_group(pw[0][:T])
    kb_pw = by_group(pw[1][:T][::-1])
    skip = jnp.eye(S5_GROUP, dtype=F32)[None] * d_skip.astype(F32).reshape(S5_GROUPS, 1, S5_GROUP)
    lag_rows = jnp.stack([lanes([kf_pw.real, -kf_pw.imag, kb_pw.real, -kb_pw.imag]),
                          lanes([-kf_pw.imag, -kf_pw.real, -kb_pw.imag, -kb_pw.real]),
                          inj_rows[:, 2],
                          jnp.pad(skip, ((0, 0), (0, 0), (0, S5_ROW - S5_GROUP)))], axis=1)

    a_chunk = pw[:, T]
    a1 = jnp.concatenate([a_chunk.real, a_chunk.real], axis=-1)
    a2 = jnp.concatenate([-a_chunk.imag, a_chunk.imag], axis=-1)
    return lag_rows, out_rows, inj_rows, a1, a2


def _factor_rows_to_matrix(r_ref, gi, dst):
    s_a = r_ref[gi, 2]
    s_b = r_ref[gi, 3]
    for a in range(S5_CHUNK):
        rows = r_ref[gi, 0, a:a + 1, :] * s_a + r_ref[gi, 1, a:a + 1, :] * s_b
        dst[gi, a * S5_GROUP:(a + 1) * S5_GROUP, :] = rows.astype(BF16)


def _s5_inject_kernel(x_ref, r_ref, sf_ref, sb_ref, m_scr):
    for gi in range(S5_GROUP_BLOCK):
        _factor_rows_to_matrix(r_ref, gi, m_scr)
        s = jnp.dot(x_ref[gi], m_scr[gi], preferred_element_type=F32)
        sf_ref[gi] = s[:, 0:S5_HALF]
        sb_ref[gi] = s[:, S5_HALF:S5_SLANES]


def _s5_inject(xg, inj_rows):
    bsz, _, nchunk, _ = xg.shape
    gb = S5_GROUP_BLOCK
    s_spec = pl.BlockSpec((None, gb, nchunk, S5_HALF), lambda b, g: (b, g, 0, 0))
    s_shape = jax.ShapeDtypeStruct((bsz, S5_GROUPS, nchunk, S5_HALF), F32)
    return pl.pallas_call(
        _s5_inject_kernel,
        grid=(bsz, S5_GROUPS // gb),
        in_specs=[
            pl.BlockSpec((None, gb, nchunk, S5_ROW), lambda b, g: (b, g, 0, 0)),
            pl.BlockSpec((gb, 4, S5_GROUP, S5_SLANES), lambda b, g: (g, 0, 0, 0)),
        ],
        out_specs=[s_spec, s_spec],
        out_shape=[s_shape, s_shape],
        scratch_shapes=[pltpu.VMEM((gb, S5_ROW, S5_SLANES), BF16)],
        compiler_params=_cparams("arbitrary", "arbitrary"),
        name="s5_inject",
    )(xg, inj_rows)


def _s5_scan_kernel(sf_ref, sb_ref, a1_ref, a2_ref, h0_ref, hf_ref, hb_ref, hfin_ref, hf_scr, hb_scr, *, cpb):
    @pl.when(pl.program_id(0) == 0)
    def _():
        hf_scr[...] = h0_ref[0]
        hb_scr[...] = h0_ref[1]

    a1f = a1_ref[0]
    a2f = a2_ref[0]
    a1b = a1_ref[1]
    a2b = a2_ref[1]
    hf = hf_scr[...]
    hb = hb_scr[...]
    for c in range(cpb):
        hf_ref[:, c * S5_HALF:(c + 1) * S5_HALF] = hf.astype(BF16)
        hf = a1f * hf + a2f * pltpu.roll(hf, S5_STATE, 1) + sf_ref[:, c * S5_HALF:(c + 1) * S5_HALF]
        cb = cpb - 1 - c
        hb_ref[:, cb * S5_HALF:(cb + 1) * S5_HALF] = hb.astype(BF16)
        hb = a1b * hb + a2b * pltpu.roll(hb, S5_STATE, 1) + sb_ref[:, cb * S5_HALF:(cb + 1) * S5_HALF]
    hf_scr[...] = hf
    hb_scr[...] = hb

    @pl.when(pl.program_id(0) == pl.num_programs(0) - 1)
    def _():
        hfin_ref[0] = hf
        hfin_ref[1] = hb


def _s5_scan(s_f, s_b, a1, a2, h0):
    bsz, _, nchunk, _ = s_f.shape
    rows = bsz * S5_GROUPS
    cpb = min(S5_SCAN_BLOCK, nchunk)
    nblk = nchunk // cpb
    fwd = pl.BlockSpec((rows, cpb * S5_HALF), lambda i: (0, i))
    bwd = pl.BlockSpec((rows, cpb * S5_HALF), lambda i: (0, nblk - 1 - i))
    small = pl.BlockSpec((2, rows, S5_HALF), lambda i: (0, 0, 0))
    hf, hb, hfin = pl.pallas_call(
        functools.partial(_s5_scan_kernel, cpb=cpb),
        grid=(nblk,),
        in_specs=[fwd, bwd, small, small, small],
        out_specs=[fwd, bwd, small],
        out_shape=[jax.ShapeDtypeStruct((rows, nchunk * S5_HALF), BF16)] * 2
        + [jax.ShapeDtypeStruct((2, rows, S5_HALF), F32)],
        scratch_shapes=[pltpu.VMEM((rows, S5_HALF), F32), pltpu.VMEM((rows, S5_HALF), F32)],
        compiler_params=_cparams("arbitrary"),
        name="s5_scan",
    )(s_f.reshape(rows, nchunk * S5_HALF), s_b.reshape(rows, nchunk * S5_HALF), a1, a2, h0)
    shape4 = (bsz, S5_GROUPS, nchunk, S5_HALF)
    return hf.reshape(shape4), hb.reshape(shape4), hfin


def _dot_nt(a, b):
    return lax.dot_general(a, b, (((1,), (1,)), ((), ())), preferred_element_type=F32)


def _s5_lag_kernel(t_ref, r_ref, o_ref):
    hp = lax.Precision.HIGHEST
    nt = (((1,), (1,)), ((), ()))
    zero_rows = jnp.zeros((S5_GROUP, S5_ROW), F32)
    for gi in range(S5_GROUP_BLOCK):
        c_re = r_ref[gi, 2]
        c_im = r_ref[gi, 3]
        c_pw = jnp.concatenate([t_ref[gi, 0, t:t + 1, :] * c_re + t_ref[gi, 1, t:t + 1, :] * c_im
                                for t in range(S5_CHUNK)], axis=0)
        b_bar = t_ref[gi, 2]
        lag_f = lax.dot_general(b_bar[:, 0:S5_HALF], c_pw[:, 0:S5_HALF], nt, precision=hp,
                                preferred_element_type=F32) + t_ref[gi, 3]
        lag_b = lax.dot_general(b_bar[:, S5_HALF:S5_SLANES], c_pw[:, S5_HALF:S5_SLANES], nt, precision=hp,
                                preferred_element_type=F32)
        o_ref[gi, 0] = jnp.concatenate([zero_rows, lag_f], axis=1)
        o_ref[gi, 1] = jnp.concatenate([lag_b, zero_rows], axis=1)


def _s5_lags(lag_rows, out_rows):
    n = lag_rows.shape[0]
    gb = S5_GROUP_BLOCK
    spec = pl.BlockSpec((gb, 4, S5_GROUP, S5_SLANES), lambda g: (g, 0, 0, 0))
    return pl.pallas_call(
        _s5_lag_kernel,
        grid=(n // gb,),
        in_specs=[spec, spec],
        out_specs=pl.BlockSpec((gb, 2, S5_GROUP, 2 * S5_ROW), lambda g: (g, 0, 0, 0)),
        out_shape=jax.ShapeDtypeStruct((n, 2, S5_GROUP, 2 * S5_ROW), F32),
        compiler_params=_cparams("arbitrary"),
        name="s5_lags",
    )(lag_rows, out_rows)


def _s5_output_kernel(x_ref, hf_ref, hb_ref, t_ref, r_ref, y_ref, m_scr, c_scr):
    for gi in range(S5_GROUP_BLOCK):
        k_f = t_ref[gi, 0]
        k_b = t_ref[gi, 1]
        for i in range(S5_CHUNK):
            lo_f = S5_ROW - i * S5_GROUP
            lo_b = (S5_CHUNK - 1 - i) * S5_GROUP
            m_scr[gi, i * S5_GROUP:(i + 1) * S5_GROUP, :] = (k_f[:, lo_f:lo_f + S5_ROW]
                                                             + k_b[:, lo_b:lo_b + S5_ROW]).astype(BF16)
        _factor_rows_to_matrix(r_ref, gi, c_scr)
        y = jnp.dot(x_ref[gi], m_scr[gi], preferred_element_type=F32)
        y = y + _dot_nt(hf_ref[gi], c_scr[gi, :, 0:S5_HALF])
        y = y + _dot_nt(hb_ref[gi], c_scr[gi, :, S5_HALF:S5_SLANES])
        y_ref[gi] = y.astype(BF16)


def _s5_output(xg, hf, hb, toep_rows, out_rows):
    bsz, _, nchunk, _ = xg.shape
    gb = S5_GROUP_BLOCK
    return pl.pallas_call(
        _s5_output_kernel,
        grid=(bsz, S5_GROUPS // gb),
        in_specs=[
            pl.BlockSpec((None, gb, nchunk, S5_ROW), lambda b, g: (b, g, 0, 0)),
            pl.BlockSpec((None, gb, nchunk, S5_HALF), lambda b, g: (b, g, 0, 0)),
            pl.BlockSpec((None, gb, nchunk, S5_HALF), lambda b, g: (b, g, 0, 0)),
            pl.BlockSpec((gb, 2, S5_GROUP, 2 * S5_ROW), lambda b, g: (g, 0, 0, 0)),
            pl.BlockSpec((gb, 4, S5_GROUP, S5_SLANES), lambda b, g: (g, 0, 0, 0)),
        ],
        out_specs=pl.BlockSpec((None, gb, nchunk, S5_ROW), lambda b, g: (b, g, 0, 0)),
        out_shape=jax.ShapeDtypeStruct((bsz, S5_GROUPS, nchunk, S5_ROW), BF16),
        scratch_shapes=[pltpu.VMEM((gb, S5_ROW, S5_ROW), BF16), pltpu.VMEM((gb, S5_ROW, S5_SLANES), BF16)],
        compiler_params=_cparams("arbitrary", "arbitrary"),
        name="s5_output",
    )(xg, hf, hb, toep_rows, out_rows)


def _s5_branch(xg_c, xg_l, mats, need_ctx):
    toep_rows, out_rows, inj_rows, a1, a2 = mats
    bsz = xg_l.shape[0]
    a1r = jnp.tile(a1, (1, bsz, 1))
    a2r = jnp.tile(a2, (1, bsz, 1))
    sf_c, sb_c = _s5_inject(xg_c, inj_rows)
    sf_l, sb_l = _s5_inject(xg_l, inj_rows)
    hf_c, hb_c, h_ctx = _s5_scan(sf_c, sb_c, a1r, a2r, jnp.zeros_like(a1r))
    hf_l, hb_l, _ = _s5_scan(sf_l, sb_l, a1r, a2r, h_ctx)
    y_l = _s5_output(xg_l, hf_l, hb_l, toep_rows, out_rows)
    y_c = _s5_output(xg_c, hf_c, hb_c, toep_rows, out_rows) if need_ctx else None
    return y_c, y_l


def _gla_kernel(qf_ref, kf_ref, vf_ref, lrf_ref, qb_ref, kb_ref, vb_ref, lrb_ref, wg_ref, bg_ref, s0_ref,
                of_ref, ob_ref, sfin_ref, s_scr):
    w = pl.program_id(0)

    @pl.when(w == 0)
    def _():
        s_scr[...] = s0_ref[...]

    row = lax.broadcasted_iota(jnp.int32, (GLA_COL, GLA_COL), 0)
    col = lax.broadcasted_iota(jnp.int32, (GLA_COL, GLA_COL), 1)
    same_chunk = (row // GLA_CHUNK) == (col // GLA_CHUNK)
    r64 = lax.broadcasted_iota(jnp.int32, (GLA_CHUNK, GLA_CHUNK), 0)
    c64 = lax.broadcasted_iota(jnp.int32, (GLA_CHUNK, GLA_CHUNK), 1)
    dirs = (
        (qf_ref, kf_ref, vf_ref, lrf_ref, of_ref, jnp.where(same_chunk & (col <= row), 1.0, 0.0).astype(BF16),
         c64 <= r64),
        (qb_ref, kb_ref, vb_ref, lrb_ref, ob_ref, jnp.where(same_chunk & (col >= row), 1.0, 0.0).astype(BF16),
         c64 >= r64),
    )
    chains = [(b, d) for b in range(s_scr.shape[0]) for d in range(2)]

    pre = []
    for b, d in chains:
        q_ref, k_ref, _, lr_ref, _, tri, _ = dirs[d]
        z = jnp.dot(lr_ref[b].astype(BF16), wg_ref[d], preferred_element_type=F32) + bg_ref[d]
        g = (jnp.minimum(z, 0.0) - jnp.log1p(jnp.exp(-jnp.abs(z)))) * (1.0 / GLA_NORMALIZER)
        g_hi = g.astype(BF16)
        g_lo = (g - g_hi.astype(F32)).astype(BF16)
        gc = jnp.dot(tri, g_hi, preferred_element_type=F32) + jnp.dot(tri, g_lo, preferred_element_type=F32)
        qt = (q_ref[b] * jnp.exp(gc) * (GLA_DK ** -0.5)).astype(BF16)
        pre.append((qt, gc.T, k_ref[b].T))

    for pos in range(2):
        keys = []
        for (b, d), (qt, gct, kt) in zip(chains, pre):
            ci = pos if d == 0 else 1 - pos
            r0 = ci * GLA_CHUNK
            last = r0 + GLA_CHUNK - 1 if d == 0 else r0
            gct_c = gct[:, r0:r0 + GLA_CHUNK]
            dec = jnp.exp(gct[:, last:last + 1])
            k_in = kt[:, r0:r0 + GLA_CHUNK] * jnp.exp(-gct_c)
            keys.append((r0, dec, k_in.astype(BF16), (k_in * dec).astype(BF16)))
        for h in range(GLA_HEADS):
            ks = slice(h * GLA_DK, (h + 1) * GLA_DK)
            vs = slice(h * GLA_DV, (h + 1) * GLA_DV)
            scores = []
            for (b, d), (qt, _, _), (r0, _, k_in, _) in zip(chains, pre, keys):
                sc = jnp.dot(qt[r0:r0 + GLA_CHUNK, ks], k_in[ks, :], preferred_element_type=F32)
                scores.append(jnp.where(dirs[d][6], sc, 0.0).astype(BF16))
            for (b, d), (qt, _, _), (r0, dec, _, k_st), sc in zip(chains, pre, keys, scores):
                v_h = dirs[d][2][b, r0:r0 + GLA_CHUNK, vs].astype(BF16)
                s_h = s_scr[b, d, ks, :]
                o_h = jnp.dot(sc, v_h, preferred_element_type=F32)
                o_h = o_h + jnp.dot(qt[r0:r0 + GLA_CHUNK, ks], s_h.astype(BF16), preferred_element_type=F32)
                dirs[d][4][b, r0:r0 + GLA_CHUNK, vs] = o_h
                s_scr[b, d, ks, :] = dec[ks, :] * s_h + jnp.dot(k_st[ks, :], v_h, preferred_element_type=F32)

    @pl.when(w == pl.num_programs(0) - 1)
    def _():
        sfin_ref[...] = s_scr[...]


def _gla(q, k, v, lr, wg, bg, s0):
    bsz, ncol = q.shape[:2]
    fwd = lambda w: (0, w, 0, 0)
    bwd = lambda w: (0, ncol - 1 - w, 0, 0)
    widths = (GLA_KEY, GLA_KEY, GLA_WIDTH, LR_PAD)
    state_spec = pl.BlockSpec((bsz, 2, GLA_KEY, GLA_DV), lambda w: (0, 0, 0, 0))
    in_specs = ([pl.BlockSpec((bsz, None, GLA_COL, wd), fwd) for wd in widths]
                + [pl.BlockSpec((bsz, None, GLA_COL, wd), bwd) for wd in widths]
                + [pl.BlockSpec((2, LR_PAD, GLA_KEY), lambda w: (0, 0, 0)),
                   pl.BlockSpec((2, 1, GLA_KEY), lambda w: (0, 0, 0)),
                   state_spec])
    o_shape = jax.ShapeDtypeStruct((bsz, ncol, GLA_COL, GLA_WIDTH), F32)
    return pl.pallas_call(
        _gla_kernel,
        grid=(ncol,),
        in_specs=in_specs,
        out_specs=[pl.BlockSpec((bsz, None, GLA_COL, GLA_WIDTH), fwd),
                   pl.BlockSpec((bsz, None, GLA_COL, GLA_WIDTH), bwd),
                   state_spec],
        out_shape=[o_shape, o_shape, jax.ShapeDtypeStruct((bsz, 2, GLA_KEY, GLA_DV), F32)],
        scratch_shapes=[pltpu.VMEM((bsz, 2, GLA_KEY, GLA_DV), F32)],
        compiler_params=_cparams("arbitrary"),
        name="gla",
    )(q, k, v, lr, q, k, v, lr, wg, bg, s0)


def _finish_kernel(ys_ref, zs_ref, of_ref, ob_ref, zg_ref, x_ref, mod_ref, wglu_ref, bglu_ref, gn_ref, wout_ref,
                   fin_ref, o_ref, ys_scr, o_scr, *, final, colmajor):
    nchunk = ys_ref.shape[1]
    for half in range(S5_CHUNK // SLOTS):
        for kt in range(S5_WIDTH // LANES):
            ws = [ys_ref[kt * SLOTS + qq, :, half * LANES:(half + 1) * LANES].astype(F32) for qq in range(SLOTS)]
            vs = _swap_slots_with_index(ws)
            for jj in range(SLOTS):
                ys_scr[kt, pl.ds(half * SLOTS + jj, nchunk, stride=CHUNK_PITCH), :] = vs[jj]
    ys = jnp.concatenate(
        [jnp.concatenate([ys_scr[kt, c * CHUNK_PITCH:c * CHUNK_PITCH + S5_CHUNK, :] for c in range(nchunk)], axis=0)
         for kt in range(S5_WIDTH // LANES)], axis=1)
    s = jax.nn.gelu(ys, approximate=True)
    t = jnp.dot(s.astype(BF16), wglu_ref[...], preferred_element_type=F32) + bglu_ref[...]
    zs = zs_ref[...].astype(F32)
    a_s = (s * jax.nn.sigmoid(t)) * (zs * jax.nn.sigmoid(zs))
    y = jnp.dot(a_s.astype(BF16), wout_ref[0:S5_WIDTH, :], preferred_element_type=F32)

    if colmajor:
        for w in range(GRID_W):
            o_w = of_ref[w] + ob_ref[w]
            for h in range(GLA_HEADS):
                o_scr[h, pl.ds(w, GRID_ROWS_PER_BLOCK, stride=GRID_ROW_PITCH), :] = o_w[:, h * GLA_DV:(h + 1) * GLA_DV]
        o_heads = [jnp.concatenate([o_scr[h, r * GRID_ROW_PITCH:r * GRID_ROW_PITCH + GRID_W, :]
                                    for r in range(GRID_ROWS_PER_BLOCK)], axis=0) for h in range(GLA_HEADS)]
    else:
        o_sum = of_ref[...] + ob_ref[...]
        o_heads = [o_sum[:, h * GLA_DV:(h + 1) * GLA_DV] for h in range(GLA_HEADS)]
    zg = zg_ref[...].astype(F32)
    gate_g = zg * jax.nn.sigmoid(zg)
    for h in range(GLA_HEADS):
        o_h = o_heads[h]
        o_n = o_h * lax.rsqrt(jnp.mean(o_h * o_h, axis=-1, keepdims=True) + EPS) * gn_ref[...]
        a_h = (o_n * gate_g[:, h * GLA_DV:(h + 1) * GLA_DV]).astype(BF16)
        y = y + jnp.dot(a_h, wout_ref[S5_WIDTH + h * GLA_DV:S5_WIDTH + (h + 1) * GLA_DV, :],
                        preferred_element_type=F32)
    gate = mod_ref[:, 2 * D_MODEL:3 * D_MODEL]
    xn = x_ref[...] + gate * y
    if final:
        ms = jnp.mean(xn * xn, axis=-1, keepdims=True)
        xn = xn * lax.rsqrt(ms + EPS) * fin_ref[...]
    o_ref[...] = xn


def _finish(ys, zs, o_f, o_b, zg, x2, mods, mod_row, w_glu, b_glu, gnorm, w_out, fin_gain, final, colmajor):
    rows = x2.shape[0]
    per_seq = rows // ys.shape[0]
    tb = min(ROW_BLOCK, per_seq)
    bps = per_seq // tb
    rb = lambda w: pl.BlockSpec((tb, w), lambda i: (i, 0))
    full = lambda a: pl.BlockSpec(a.shape, lambda i: (0,) * a.ndim)
    if colmajor:
        assert tb == ROW_BLOCK
        o_spec = pl.BlockSpec((None, GRID_W, GRID_ROWS_PER_BLOCK, GLA_WIDTH), lambda i: (i // bps, 0, i % bps, 0))
    else:
        o_spec = rb(GLA_WIDTH)
    return pl.pallas_call(
        functools.partial(_finish_kernel, final=final, colmajor=colmajor),
        grid=(rows // tb,),
        in_specs=[pl.BlockSpec((None, S5_GROUPS, tb // S5_CHUNK, S5_ROW), lambda i: (i // bps, 0, i % bps, 0)),
                  rb(S5_WIDTH), o_spec, o_spec, rb(GLA_WIDTH), rb(D_MODEL),
                  pl.BlockSpec((None, 1, 3 * D_MODEL), lambda i: (mod_row(i), 0, 0)),
                  full(w_glu), full(b_glu), full(gnorm), full(w_out), full(fin_gain)],
        out_specs=rb(D_MODEL),
        out_shape=jax.ShapeDtypeStruct((rows, D_MODEL), F32),
        scratch_shapes=[pltpu.VMEM((S5_WIDTH // LANES, tb // S5_CHUNK * CHUNK_PITCH, LANES), F32),
                        pltpu.VMEM((GLA_HEADS, GRID_ROWS_PER_BLOCK * GRID_ROW_PITCH, GLA_DV), F32)],
        compiler_params=_cparams("arbitrary"),
        name="finish",
    )(ys, zs, o_f, o_b, zg, x2, mods, w_glu, b_glu, gnorm, w_out, fin_gain)


def _layer(x_lat, x_ctx, mods, lw, need_ctx_out, final):
    bsz, l, _ = x_lat.shape
    lc = x_ctx.shape[1]
    blocks_per_batch = l // ROW_BLOCK
    lat_row = lambda i: i // blocks_per_batch
    ctx_row = lambda i: bsz

    pj_l = _inproj(x_lat.reshape(bsz * l, D_MODEL), mods, lw["gain"], lw["w_in"], lat_row, bsz, colmajor=True)
    pj_c = _inproj(x_ctx.reshape(bsz * lc, D_MODEL), mods, lw["gain"], lw["w_in"], ctx_row, bsz, colmajor=False)
    xg_l, zs_l, q_l, k_l, v_l, zg_l, lr_l = pj_l
    xg_c, zs_c, q_c, k_c, v_c, zg_c, lr_c = pj_c

    ys_c, ys_l = _s5_branch(xg_c, xg_l, lw["s5"], need_ctx_out)

    cols = lambda t: t.reshape(bsz, lc // GLA_COL, GLA_COL, t.shape[-1])
    s_zero = jnp.zeros((bsz, 2, GLA_KEY, GLA_DV), F32)
    ocf, ocb, s_ctx = _gla(cols(q_c), cols(k_c), cols(v_c), cols(lr_c), lw["wg"], lw["bg"], s_zero)
    olf, olb, _ = _gla(q_l, k_l, v_l, lr_l, lw["wg"], lw["bg"], s_ctx)

    fin_args = (lw["w_glu"], lw["b_glu"], lw["gnorm"], lw["w_out"], lw["fin"])
    x_lat_new = _finish(ys_l, zs_l, olf, olb, zg_l, x_lat.reshape(bsz * l, D_MODEL), mods, lat_row,
                        *fin_args, final=final, colmajor=True).reshape(bsz, l, D_MODEL)
    x_ctx_new = x_ctx
    if need_ctx_out:
        x_ctx_new = _finish(ys_c, zs_c, ocf.reshape(bsz * lc, GLA_WIDTH), ocb.reshape(bsz * lc, GLA_WIDTH), zg_c,
                            x_ctx.reshape(bsz * lc, D_MODEL), mods, ctx_row, *fin_args, final=False,
                            colmajor=False).reshape(bsz, lc, D_MODEL)
    return x_lat_new, x_ctx_new


def _stack_weights(norm_g, w_in, s5_lam_re, s5_lam_im, s5_log_dt, s5_b_re, s5_b_im, s5_c_re, s5_c_im, s5_d,
                   s5_w_glu, s5_b_glu, gla_w_gate, gla_b_gate, gla_norm_g, w_out, final_norm):
    w_pad = jnp.pad(w_in, ((0, 0), (0, 0), (0, IN_PAD - w_in.shape[-1]))).astype(BF16)
    wg = jnp.zeros((DEPTH, 2, LR_PAD, GLA_KEY), F32)
    for d in range(2):
        wg = wg.at[:, d, d * GLA_RANK:(d + 1) * GLA_RANK].set(gla_w_gate[:, d])
    lag_rows, out_rows, inj_rows, a1, a2 = jax.vmap(_s5_matrices)(
        s5_lam_re, s5_lam_im, s5_log_dt, s5_b_re, s5_b_im, s5_c_re, s5_c_im, s5_d)
    merged = lambda t: t.reshape((DEPTH * S5_GROUPS,) + t.shape[2:])
    toep_rows = _s5_lags(merged(lag_rows), merged(out_rows)).reshape(DEPTH, S5_GROUPS, 2, S5_GROUP, 2 * S5_ROW)
    s5 = (toep_rows, out_rows, inj_rows, a1, a2)
    wg = wg.astype(BF16)
    w_glu = s5_w_glu.astype(BF16)
    w_o = w_out.astype(BF16)
    return [{
        "gain": norm_g[i].reshape(1, D_MODEL),
        "w_in": w_pad[i],
        "s5": tuple(m[i] for m in s5),
        "wg": wg[i],
        "bg": gla_b_gate[i].reshape(2, 1, GLA_KEY),
        "w_glu": w_glu[i],
        "b_glu": s5_b_glu[i].reshape(1, S5_WIDTH),
        "gnorm": gla_norm_g[i].reshape(1, GLA_DV),
        "w_out": w_o[i],
        "fin": final_norm.reshape(1, D_MODEL),
    } for i in range(DEPTH)]


def kernel(x, c, ctx, c_ctx, norm_g, w_mod, b_mod, w_in, s5_lam_re, s5_lam_im, s5_log_dt, s5_b_re, s5_b_im,
           s5_c_re, s5_c_im, s5_d, s5_w_glu, s5_b_glu, gla_w_gate, gla_b_gate, gla_norm_g, w_out, final_norm):
    bsz = x.shape[0]
    cond8 = jnp.concatenate([c, c_ctx[None], jnp.zeros((8 - bsz - 1, D_MODEL), F32)], axis=0)
    mods_all = _modulation(cond8, w_mod, b_mod)
    weights = _stack_weights(norm_g, w_in, s5_lam_re, s5_lam_im, s5_log_dt, s5_b_re, s5_b_im, s5_c_re, s5_c_im,
                             s5_d, s5_w_glu, s5_b_glu, gla_w_gate, gla_b_gate, gla_norm_g, w_out, final_norm)
    x_lat, x_ctx = x, ctx
    for i in range(DEPTH):
        mods = mods_all[i].reshape(8, 1, 3 * D_MODEL)
        last = i == DEPTH - 1
        x_lat, x_ctx = _layer(x_lat, x_ctx, mods, weights[i], need_ctx_out=not last, final=last)
    return x_lat
```

```python
import functools

import jax
import jax.numpy as jnp
from jax import lax
from jax.experimental import pallas as pl
from jax.experimental.pallas import tpu as pltpu

F32 = jnp.float32
BF16 = jnp.bfloat16

D_MODEL = 1024
DEPTH = 4
GRID_W = 64
EPS = 1e-6
LANES = 128
SUBLANES = 8
S5_WIDTH = 512
S5_GROUP = 16
S5_GROUPS = 32
S5_STATE = 64
S5_CHUNK = 16
S5_ROW = S5_CHUNK * S5_GROUP
S5_HALF = 2 * S5_STATE
S5_SLANES = 2 * S5_HALF
S5_SCAN_BLOCK = 32
S5_GROUP_BLOCK = 8
SLOTS = LANES // S5_GROUP
GLA_HEADS = 4
GLA_DV = 128
GLA_DK = 64
GLA_WIDTH = 512
GLA_KEY = 256
GLA_RANK = 16
GLA_NORMALIZER = 16.0
GLA_CHUNK = 64
GLA_COL = 128
LR_PAD = 128
IN_PAD = 2 * S5_WIDTH + 2 * GLA_KEY + 2 * GLA_WIDTH + LR_PAD

ROW_BLOCK = 512
GRID_ROWS_PER_BLOCK = ROW_BLOCK // GRID_W
FINISH_BLOCK = 1024
CHUNK_PITCH = S5_CHUNK + SUBLANES
GRID_ROW_PITCH = GRID_W + SUBLANES
VMEM_LIMIT = 48 * 1024 * 1024


def _cparams(*sem):
    return pltpu.CompilerParams(dimension_semantics=sem, vmem_limit_bytes=VMEM_LIMIT)


def _mod_kernel(cond_ref, w_ref, b_ref, o_ref):
    cnd = cond_ref[...]
    a = cnd * jax.nn.sigmoid(cnd)
    o_ref[...] = jnp.dot(a, w_ref[...], preferred_element_type=F32,
                         precision=lax.Precision.HIGHEST) + b_ref[...]


def _modulation(cond8, w_mod, b_mod):
    nblk = 3
    return pl.pallas_call(
        _mod_kernel,
        grid=(DEPTH, nblk),
        in_specs=[
            pl.BlockSpec((8, D_MODEL), lambda l, j: (0, 0)),
            pl.BlockSpec((None, D_MODEL, D_MODEL), lambda l, j: (l, 0, j)),
            pl.BlockSpec((None, 1, D_MODEL), lambda l, j: (l, 0, j)),
        ],
        out_specs=pl.BlockSpec((None, 8, D_MODEL), lambda l, j: (l, 0, j)),
        out_shape=jax.ShapeDtypeStruct((DEPTH, 8, 3 * D_MODEL), F32),
        compiler_params=_cparams("arbitrary", "arbitrary"),
        name="modulation",
    )(cond8, w_mod, b_mod.reshape(DEPTH, 1, 3 * D_MODEL))


_IN_SEGS = (("zs", S5_WIDTH, False), ("q", GLA_KEY, True), ("k", GLA_KEY, True), ("v", GLA_WIDTH, True),
            ("zg", GLA_WIDTH, False), ("lr", LR_PAD, True))
_BF16_SEGS = ("zs", "zg")


def _swap_slots_with_index(vs):
    lane = lax.broadcasted_iota(jnp.int32, vs[0].shape, 1)
    for d in (4, 2, 1):
        sh = d * S5_GROUP
        low = (lane & sh) == 0
        nxt = list(vs)
        for i in range(SLOTS):
            if i & d == 0:
                a, b = vs[i], vs[i + d]
                nxt[i] = jnp.where(low, a, pltpu.roll(b, sh, 1))
                nxt[i + d] = jnp.where(low, pltpu.roll(a, LANES - sh, 1), b)
        vs = nxt
    return vs


def _inproj_kernel(x_ref, mod_ref, gain_ref, w_ref, xg_ref, *rest, colmajor):
    out_refs, u_scr, c_scr = rest[:-2], rest[-2], rest[-1]
    x = x_ref[...]
    ms = jnp.mean(x * x, axis=-1, keepdims=True)
    y = x * lax.rsqrt(ms + EPS) * gain_ref[...]
    shift = mod_ref[:, 0:D_MODEL]
    scale = mod_ref[:, D_MODEL:2 * D_MODEL]
    h = (y * (1.0 + scale) + shift).astype(BF16)
    u = jnp.dot(h, w_ref[:, 0:S5_WIDTH], preferred_element_type=F32)
    nchunk = u.shape[0] // S5_CHUNK
    for kt in range(S5_WIDTH // LANES):
        for c in range(nchunk):
            u_scr[kt, c * CHUNK_PITCH:c * CHUNK_PITCH + S5_CHUNK, :] = u[c * S5_CHUNK:(c + 1) * S5_CHUNK,
                                                                         kt * LANES:(kt + 1) * LANES]
    off = S5_WIDTH
    for (_, width, cm), o_ref in zip(_IN_SEGS, out_refs):
        val = jnp.dot(h, w_ref[:, off:off + width], preferred_element_type=F32)
        off += width
        if colmajor and cm:
            for kt in range(width // LANES):
                for r in range(GRID_ROWS_PER_BLOCK):
                    c_scr[kt, r * GRID_ROW_PITCH:r * GRID_ROW_PITCH + GRID_W, :] = val[r * GRID_W:(r + 1) * GRID_W,
                                                                                       kt * LANES:(kt + 1) * LANES]
            for w in range(GRID_W):
                for kt in range(width // LANES):
                    o_ref[w, :, kt * LANES:(kt + 1) * LANES] = c_scr[kt, pl.ds(w, GRID_ROWS_PER_BLOCK,
                                                                              stride=GRID_ROW_PITCH), :]
        else:
            o_ref[...] = val.astype(o_ref.dtype)
    for half in range(S5_CHUNK // SLOTS):
        for kt in range(S5_WIDTH // LANES):
            vs = [u_scr[kt, pl.ds(half * SLOTS + jj, nchunk, stride=CHUNK_PITCH), :] for jj in range(SLOTS)]
            ws = _swap_slots_with_index(vs)
            for qq in range(SLOTS):
                xg_ref[kt * SLOTS + qq, :, half * LANES:(half + 1) * LANES] = ws[qq].astype(BF16)


def _inproj(x2, mods, gain, w_pad, ctx_mod_row, bsz, colmajor):
    rows = x2.shape[0]
    per_seq = rows // bsz
    tb = min(ROW_BLOCK, per_seq)
    bps = per_seq // tb
    mod_row = (lambda i: i // bps) if ctx_mod_row is None else (lambda i: ctx_mod_row)
    out_specs = [pl.BlockSpec((None, S5_GROUPS, tb // S5_CHUNK, S5_ROW), lambda i: (i // bps, 0, i % bps, 0))]
    out_shape = [jax.ShapeDtypeStruct((bsz, S5_GROUPS, per_seq // S5_CHUNK, S5_ROW), BF16)]
    for name, w, cm in _IN_SEGS:
        if colmajor and cm:
            assert tb == ROW_BLOCK
            out_specs.append(pl.BlockSpec((None, GRID_W, GRID_ROWS_PER_BLOCK, w),
                                          lambda i: (i // bps, 0, i % bps, 0)))
            out_shape.append(jax.ShapeDtypeStruct((bsz, GRID_W, per_seq // GRID_W, w), F32))
        else:
            out_specs.append(pl.BlockSpec((tb, w), lambda i: (i, 0)))
            out_shape.append(jax.ShapeDtypeStruct((rows, w), BF16 if name in _BF16_SEGS else F32))
    return pl.pallas_call(
        functools.partial(_inproj_kernel, colmajor=colmajor),
        grid=(rows // tb,),
        in_specs=[
            pl.BlockSpec((tb, D_MODEL), lambda i: (i, 0)),
            pl.BlockSpec((None, 1, 3 * D_MODEL), lambda i: (mod_row(i), 0, 0)),
            pl.BlockSpec((1, D_MODEL), lambda i: (0, 0)),
            pl.BlockSpec((D_MODEL, IN_PAD), lambda i: (0, 0)),
        ],
        out_specs=out_specs,
        out_shape=out_shape,
        scratch_shapes=[pltpu.VMEM((S5_WIDTH // LANES, tb // S5_CHUNK * CHUNK_PITCH, LANES), F32),
                        pltpu.VMEM((GLA_WIDTH // LANES, GRID_ROWS_PER_BLOCK * GRID_ROW_PITCH, LANES), F32)],
        compiler_params=_cparams("arbitrary"),
        name="inproj",
    )(x2, mods, gain, w_pad)


def _s5_matrices(lam_re, lam_im, log_dt, b_re, b_im, c_re, c_im, d_skip):
    T = S5_CHUNK
    lam = lax.complex(lam_re.astype(F32), lam_im.astype(F32))
    dt = jnp.exp(log_dt.astype(F32))[..., None]
    ldt = lam * dt
    lam_bar = jnp.exp(ldt)
    b_c = lax.complex(b_re.astype(F32), b_im.astype(F32))
    b_bar = ((lam_bar - 1.0) / lam)[..., None] * b_c[None]
    c_c = lax.complex(c_re.astype(F32), c_im.astype(F32))
    steps = jnp.arange(T + 1, dtype=F32)
    pw = jnp.exp(ldt[:, None] * steps[None, :, None, None].astype(jnp.complex64))

    lanes = lambda parts: jnp.concatenate(parts, axis=-1)
    by_group = lambda p: p.transpose(1, 0, 2)

    qf = by_group(pw[0][:T][::-1])
    qb = by_group(pw[1][:T])
    bt = b_bar.transpose(0, 1, 3, 2)
    inj_rows = jnp.stack([lanes([qf.real, qf.real, qb.real, qb.real]),
                          lanes([-qf.imag, qf.imag, -qb.imag, qb.imag]),
                          lanes([bt[0].real, bt[0].imag, bt[1].real, bt[1].imag]),
                          lanes([bt[0].imag, bt[0].real, bt[1].imag, bt[1].real])], axis=1)

    pf = by_group(pw[0][1:T + 1])
    pb = by_group(pw[1][::-1][:T])
    out_rows = jnp.stack([lanes([pf.real, -pf.imag, pb.real, -pb.imag]),
                          lanes([-pf.imag, -pf.real, -pb.imag, -pb.real]),
                          lanes([c_c.real] * 4),
                          lanes([c_c.imag] * 4)], axis=1)

    kf_pw = by_group(pw[0][:T])
    kb_pw = by_group(pw[1][:T][::-1])
    skip = jnp.eye(S5_GROUP, dtype=F32)[None] * d_skip.astype(F32).reshape(S5_GROUPS, 1, S5_GROUP)
    lag_rows = jnp.stack([lanes([kf_pw.real, -kf_pw.imag, kb_pw.real, -kb_pw.imag]),
                          lanes([-kf_pw.imag, -kf_pw.real, -kb_pw.imag, -kb_pw.real]),
                          inj_rows[:, 2],
                          jnp.pad(skip, ((0, 0), (0, 0), (0, S5_ROW - S5_GROUP)))], axis=1)

    a_chunk = pw[:, T]
    a1 = jnp.concatenate([a_chunk.real, a_chunk.real], axis=-1)
    a2 = jnp.concatenate([-a_chunk.imag, a_chunk.imag], axis=-1)
    return lag_rows, out_rows, inj_rows, a1, a2


def _factor_rows_to_matrix(r_ref, gi, dst):
    s_a = r_ref[gi, 2]
    s_b = r_ref[gi, 3]
    for a in range(S5_CHUNK):
        rows = r_ref[gi, 0, a:a + 1, :] * s_a + r_ref[gi, 1, a:a + 1, :] * s_b
        dst[gi, a * S5_GROUP:(a + 1) * S5_GROUP, :] = rows.astype(BF16)


def _s5_inject_kernel(x_ref, r_ref, sf_ref, sb_ref, m_scr):
    for gi in range(S5_GROUP_BLOCK):
        _factor_rows_to_matrix(r_ref, gi, m_scr)
        s = jnp.dot(x_ref[gi], m_scr[gi], preferred_element_type=F32)
        sf_ref[gi] = s[:, 0:S5_HALF]
        sb_ref[gi] = s[:, S5_HALF:S5_SLANES]


def _s5_inject(xg, inj_rows):
    bsz, _, nchunk, _ = xg.shape
    gb = S5_GROUP_BLOCK
    s_spec = pl.BlockSpec((None, gb, nchunk, S5_HALF), lambda b, g: (b, g, 0, 0))
    s_shape = jax.ShapeDtypeStruct((bsz, S5_GROUPS, nchunk, S5_HALF), F32)
    return pl.pallas_call(
        _s5_inject_kernel,
        grid=(bsz, S5_GROUPS // gb),
        in_specs=[
            pl.BlockSpec((None, gb, nchunk, S5_ROW), lambda b, g: (b, g, 0, 0)),
            pl.BlockSpec((gb, 4, S5_GROUP, S5_SLANES), lambda b, g: (g, 0, 0, 0)),
        ],
        out_specs=[s_spec, s_spec],
        out_shape=[s_shape, s_shape],
        scratch_shapes=[pltpu.VMEM((gb, S5_ROW, S5_SLANES), BF16)],
        compiler_params=_cparams("arbitrary", "arbitrary"),
        name="s5_inject",
    )(xg, inj_rows)


def _s5_scan_kernel(sf_ref, sb_ref, a1_ref, a2_ref, h0_ref, hf_ref, hb_ref, hfin_ref, hf_scr, hb_scr, *, cpb):
    @pl.when(pl.program_id(0) == 0)
    def _():
        hf_scr[...] = h0_ref[0]
        hb_scr[...] = h0_ref[1]

    swap = lambda t: pltpu.roll(t, S5_STATE, 1)
    a1f = a1_ref[0]
    a2f = a2_ref[0]
    a1b = a1_ref[1]
    a2b = a2_ref[1]
    hf = hf_scr[...]
    hb = hb_scr[...]
    hf_sw = swap(hf)
    hb_sw = swap(hb)
    for c in range(cpb):
        hf_ref[:, c * S5_HALF:(c + 1) * S5_HALF] = hf.astype(BF16)
        s_f = sf_ref[:, c * S5_HALF:(c + 1) * S5_HALF]
        hf, hf_sw = a1f * hf + a2f * hf_sw + s_f, a1f * hf_sw - a2f * hf + swap(s_f)
        cb = cpb - 1 - c
        hb_ref[:, cb * S5_HALF:(cb + 1) * S5_HALF] = hb.astype(BF16)
        s_b = sb_ref[:, cb * S5_HALF:(cb + 1) * S5_HALF]
        hb, hb_sw = a1b * hb + a2b * hb_sw + s_b, a1b * hb_sw - a2b * hb + swap(s_b)
    hf_scr[...] = hf
    hb_scr[...] = hb

    @pl.when(pl.program_id(0) == pl.num_programs(0) - 1)
    def _():
        hfin_ref[0] = hf
        hfin_ref[1] = hb


def _s5_scan(s_f, s_b, a1, a2, h0):
    bsz, _, nchunk, _ = s_f.shape
    rows = bsz * S5_GROUPS
    cpb = min(S5_SCAN_BLOCK, nchunk)
    nblk = nchunk // cpb
    fwd = pl.BlockSpec((rows, cpb * S5_HALF), lambda i: (0, i))
    bwd = pl.BlockSpec((rows, cpb * S5_HALF), lambda i: (0, nblk - 1 - i))
    small = pl.BlockSpec((2, rows, S5_HALF), lambda i: (0, 0, 0))
    hf, hb, hfin = pl.pallas_call(
        functools.partial(_s5_scan_kernel, cpb=cpb),
        grid=(nblk,),
        in_specs=[fwd, bwd, small, small, small],
        out_specs=[fwd, bwd, small],
        out_shape=[jax.ShapeDtypeStruct((rows, nchunk * S5_HALF), BF16)] * 2
        + [jax.ShapeDtypeStruct((2, rows, S5_HALF), F32)],
        scratch_shapes=[pltpu.VMEM((rows, S5_HALF), F32), pltpu.VMEM((rows, S5_HALF), F32)],
        compiler_params=_cparams("arbitrary"),
        name="s5_scan",
    )(s_f.reshape(rows, nchunk * S5_HALF), s_b.reshape(rows, nchunk * S5_HALF), a1, a2, h0)
    shape4 = (bsz, S5_GROUPS, nchunk, S5_HALF)
    return hf.reshape(shape4), hb.reshape(shape4), hfin


def _dot_nt(a, b):
    return lax.dot_general(a, b, (((1,), (1,)), ((), ())), preferred_element_type=F32)


def _s5_lag_kernel(t_ref, r_ref, o_ref):
    hp = lax.Precision.HIGHEST
    nt = (((1,), (1,)), ((), ()))
    zero_rows = jnp.zeros((S5_GROUP, S5_ROW), F32)
    for gi in range(S5_GROUP_BLOCK):
        c_re = r_ref[gi, 2]
        c_im = r_ref[gi, 3]
        c_pw = jnp.concatenate([t_ref[gi, 0, t:t + 1, :] * c_re + t_ref[gi, 1, t:t + 1, :] * c_im
                                for t in range(S5_CHUNK)], axis=0)
        b_bar = t_ref[gi, 2]
        lag_f = lax.dot_general(b_bar[:, 0:S5_HALF], c_pw[:, 0:S5_HALF], nt, precision=hp,
                                preferred_element_type=F32) + t_ref[gi, 3]
        lag_b = lax.dot_general(b_bar[:, S5_HALF:S5_SLANES], c_pw[:, S5_HALF:S5_SLANES], nt, precision=hp,
                                preferred_element_type=F32)
        o_ref[gi, 0] = jnp.concatenate([zero_rows, lag_f], axis=1)
        o_ref[gi, 1] = jnp.concatenate([lag_b, zero_rows], axis=1)


def _s5_lags(lag_rows, out_rows):
    n = lag_rows.shape[0]
    gb = S5_GROUP_BLOCK
    spec = pl.BlockSpec((gb, 4, S5_GROUP, S5_SLANES), lambda g: (g, 0, 0, 0))
    return pl.pallas_call(
        _s5_lag_kernel,
        grid=(n // gb,),
        in_specs=[spec, spec],
        out_specs=pl.BlockSpec((gb, 2, S5_GROUP, 2 * S5_ROW), lambda g: (g, 0, 0, 0)),
        out_shape=jax.ShapeDtypeStruct((n, 2, S5_GROUP, 2 * S5_ROW), F32),
        compiler_params=_cparams("arbitrary"),
        name="s5_lags",
    )(lag_rows, out_rows)


def _s5_output_kernel(x_ref, hf_ref, hb_ref, t_ref, r_ref, y_ref, m_scr, c_scr):
    for gi in range(S5_GROUP_BLOCK):
        k_f = t_ref[gi, 0]
        k_b = t_ref[gi, 1]
        for i in range(S5_CHUNK):
            lo_f = S5_ROW - i * S5_GROUP
            lo_b = (S5_CHUNK - 1 - i) * S5_GROUP
            m_scr[gi, i * S5_GROUP:(i + 1) * S5_GROUP, :] = (k_f[:, lo_f:lo_f + S5_ROW]
                                                             + k_b[:, lo_b:lo_b + S5_ROW]).astype(BF16)
        _factor_rows_to_matrix(r_ref, gi, c_scr)
        y = jnp.dot(x_ref[gi], m_scr[gi], preferred_element_type=F32)
        y = y + _dot_nt(hf_ref[gi], c_scr[gi, :, 0:S5_HALF])
        y = y + _dot_nt(hb_ref[gi], c_scr[gi, :, S5_HALF:S5_SLANES])
        y_ref[gi] = y.astype(BF16)


def _s5_output(xg, hf, hb, toep_rows, out_rows):
    bsz, _, nchunk, _ = xg.shape
    gb = S5_GROUP_BLOCK
    return pl.pallas_call(
        _s5_output_kernel,
        grid=(bsz, S5_GROUPS // gb),
        in_specs=[
            pl.BlockSpec((None, gb, nchunk, S5_ROW), lambda b, g: (b, g, 0, 0)),
            pl.BlockSpec((None, gb, nchunk, S5_HALF), lambda b, g: (b, g, 0, 0)),
            pl.BlockSpec((None, gb, nchunk, S5_HALF), lambda b, g: (b, g, 0, 0)),
            pl.BlockSpec((gb, 2, S5_GROUP, 2 * S5_ROW), lambda b, g: (g, 0, 0, 0)),
            pl.BlockSpec((gb, 4, S5_GROUP, S5_SLANES), lambda b, g: (g, 0, 0, 0)),
        ],
        out_specs=pl.BlockSpec((None, gb, nchunk, S5_ROW), lambda b, g: (b, g, 0, 0)),
        out_shape=jax.ShapeDtypeStruct((bsz, S5_GROUPS, nchunk, S5_ROW), BF16),
        scratch_shapes=[pltpu.VMEM((gb, S5_ROW, S5_ROW), BF16), pltpu.VMEM((gb, S5_ROW, S5_SLANES), BF16)],
        compiler_params=_cparams("arbitrary", "arbitrary"),
        name="s5_output",
    )(xg, hf, hb, toep_rows, out_rows)


def _s5_branch(xg_c, xg_l, mats, need_ctx):
    toep_rows, out_rows, inj_rows, a1, a2 = mats
    bsz = xg_l.shape[0]
    a1r = jnp.tile(a1, (1, bsz, 1))
    a2r = jnp.tile(a2, (1, bsz, 1))
    sf_c, sb_c = _s5_inject(xg_c, inj_rows)
    sf_l, sb_l = _s5_inject(xg_l, inj_rows)
    hf_c, hb_c, h_ctx = _s5_scan(sf_c, sb_c, a1r, a2r, jnp.zeros_like(a1r))
    hf_l, hb_l, _ = _s5_scan(sf_l, sb_l, a1r, a2r, h_ctx)
    y_l = _s5_output(xg_l, hf_l, hb_l, toep_rows, out_rows)
    y_c = _s5_output(xg_c, hf_c, hb_c, toep_rows, out_rows) if need_ctx else None
    return y_c, y_l


def _gla_kernel(qf_ref, kf_ref, vf_ref, lrf_ref, qb_ref, kb_ref, vb_ref, lrb_ref, wg_ref, bg_ref, s0_ref,
                of_ref, ob_ref, sfin_ref, s_scr):
    w = pl.program_id(0)

    @pl.when(w == 0)
    def _():
        s_scr[...] = s0_ref[...]

    row = lax.broadcasted_iota(jnp.int32, (GLA_COL, GLA_COL), 0)
    col = lax.broadcasted_iota(jnp.int32, (GLA_COL, GLA_COL), 1)
    same_chunk = (row // GLA_CHUNK) == (col // GLA_CHUNK)
    r64 = lax.broadcasted_iota(jnp.int32, (GLA_CHUNK, GLA_CHUNK), 0)
    c64 = lax.broadcasted_iota(jnp.int32, (GLA_CHUNK, GLA_CHUNK), 1)
    dirs = (
        (qf_ref, kf_ref, vf_ref, lrf_ref, of_ref, jnp.where(same_chunk & (col <= row), 1.0, 0.0).astype(BF16),
         c64 <= r64),
        (qb_ref, kb_ref, vb_ref, lrb_ref, ob_ref, jnp.where(same_chunk & (col >= row), 1.0, 0.0).astype(BF16),
         c64 >= r64),
    )
    chains = [(b, d) for b in range(s_scr.shape[0]) for d in range(2)]

    pre = []
    for b, d in chains:
        q_ref, k_ref, _, lr_ref, _, tri, _ = dirs[d]
        z = jnp.dot(lr_ref[b].astype(BF16), wg_ref[d], preferred_element_type=F32) + bg_ref[d]
        g = (jnp.minimum(z, 0.0) - jnp.log1p(jnp.exp(-jnp.abs(z)))) * (1.0 / GLA_NORMALIZER)
        g_hi = g.astype(BF16)
        g_lo = (g - g_hi.astype(F32)).astype(BF16)
        gc = jnp.dot(tri, g_hi, preferred_element_type=F32) + jnp.dot(tri, g_lo, preferred_element_type=F32)
        qt = (q_ref[b] * jnp.exp(gc) * (GLA_DK ** -0.5)).astype(BF16)
        pre.append((qt, gc.T, k_ref[b].T))

    for pos in range(2):
        keys = []
        for (b, d), (qt, gct, kt) in zip(chains, pre):
            ci = pos if d == 0 else 1 - pos
            r0 = ci * GLA_CHUNK
            last = r0 + GLA_CHUNK - 1 if d == 0 else r0
            gct_c = gct[:, r0:r0 + GLA_CHUNK]
            dec = jnp.exp(gct[:, last:last + 1])
            k_in = kt[:, r0:r0 + GLA_CHUNK] * jnp.exp(-gct_c)
            keys.append((r0, dec, k_in.astype(BF16), (k_in * dec).astype(BF16)))
        for h in range(GLA_HEADS):
            ks = slice(h * GLA_DK, (h + 1) * GLA_DK)
            vs = slice(h * GLA_DV, (h + 1) * GLA_DV)
            scores = []
            for (b, d), (qt, _, _), (r0, _, k_in, _) in zip(chains, pre, keys):
                sc = jnp.dot(qt[r0:r0 + GLA_CHUNK, ks], k_in[ks, :], preferred_element_type=F32)
                scores.append(jnp.where(dirs[d][6], sc, 0.0).astype(BF16))
            for (b, d), (qt, _, _), (r0, dec, _, k_st), sc in zip(chains, pre, keys, scores):
                v_h = dirs[d][2][b, r0:r0 + GLA_CHUNK, vs].astype(BF16)
                s_h = s_scr[b, d, ks, :]
                o_h = jnp.dot(sc, v_h, preferred_element_type=F32)
                o_h = o_h + jnp.dot(qt[r0:r0 + GLA_CHUNK, ks], s_h.astype(BF16), preferred_element_type=F32)
                dirs[d][4][b, r0:r0 + GLA_CHUNK, vs] = o_h.astype(BF16)
                s_scr[b, d, ks, :] = dec[ks, :] * s_h + jnp.dot(k_st[ks, :], v_h, preferred_element_type=F32)

    @pl.when(w == pl.num_programs(0) - 1)
    def _():
        sfin_ref[...] = s_scr[...]


def _gla(q, k, v, lr, wg, bg, s0):
    bsz, ncol = q.shape[:2]
    fwd = lambda w: (0, w, 0, 0)
    bwd = lambda w: (0, ncol - 1 - w, 0, 0)
    widths = (GLA_KEY, GLA_KEY, GLA_WIDTH, LR_PAD)
    state_spec = pl.BlockSpec((bsz, 2, GLA_KEY, GLA_DV), lambda w: (0, 0, 0, 0))
    in_specs = ([pl.BlockSpec((bsz, None, GLA_COL, wd), fwd) for wd in widths]
                + [pl.BlockSpec((bsz, None, GLA_COL, wd), bwd) for wd in widths]
                + [pl.BlockSpec((2, LR_PAD, GLA_KEY), lambda w: (0, 0, 0)),
                   pl.BlockSpec((2, 1, GLA_KEY), lambda w: (0, 0, 0)),
                   state_spec])
    o_shape = jax.ShapeDtypeStruct((bsz, ncol, GLA_COL, GLA_WIDTH), BF16)
    return pl.pallas_call(
        _gla_kernel,
        grid=(ncol,),
        in_specs=in_specs,
        out_specs=[pl.BlockSpec((bsz, None, GLA_COL, GLA_WIDTH), fwd),
                   pl.BlockSpec((bsz, None, GLA_COL, GLA_WIDTH), bwd),
                   state_spec],
        out_shape=[o_shape, o_shape, jax.ShapeDtypeStruct((bsz, 2, GLA_KEY, GLA_DV), F32)],
        scratch_shapes=[pltpu.VMEM((bsz, 2, GLA_KEY, GLA_DV), F32)],
        compiler_params=_cparams("arbitrary"),
        name="gla",
    )(q, k, v, lr, q, k, v, lr, wg, bg, s0)


def _finish_kernel(ys_ref, zs_ref, of_ref, ob_ref, zg_ref, x_ref, mod_ref, wglu_ref, bglu_ref, gn_ref, wout_ref,
                   fin_ref, o_ref, ys_scr, o_scr, *, final, colmajor):
    nchunk = ys_ref.shape[1]
    for half in range(S5_CHUNK // SLOTS):
        for kt in range(S5_WIDTH // LANES):
            ws = [ys_ref[kt * SLOTS + qq, :, half * LANES:(half + 1) * LANES].astype(F32) for qq in range(SLOTS)]
            vs = _swap_slots_with_index(ws)
            for jj in range(SLOTS):
                ys_scr[kt, pl.ds(half * SLOTS + jj, nchunk, stride=CHUNK_PITCH), :] = vs[jj]
    ys = jnp.concatenate(
        [jnp.concatenate([ys_scr[kt, c * CHUNK_PITCH:c * CHUNK_PITCH + S5_CHUNK, :] for c in range(nchunk)], axis=0)
         for kt in range(S5_WIDTH // LANES)], axis=1)
    s = jax.nn.gelu(ys, approximate=True)
    t = jnp.dot(s.astype(BF16), wglu_ref[...], preferred_element_type=F32) + bglu_ref[...]
    zs = zs_ref[...].astype(F32)
    a_s = (s * jax.nn.sigmoid(t)) * (zs * jax.nn.sigmoid(zs))
    y = jnp.dot(a_s.astype(BF16), wout_ref[0:S5_WIDTH, :], preferred_element_type=F32)

    if colmajor:
        grid_rows = of_ref.shape[1]
        for w in range(GRID_W):
            o_w = of_ref[w].astype(F32) + ob_ref[w].astype(F32)
            for h in range(GLA_HEADS):
                o_scr[h, pl.ds(w, grid_rows, stride=GRID_ROW_PITCH), :] = o_w[:, h * GLA_DV:(h + 1) * GLA_DV]
        o_heads = [jnp.concatenate([o_scr[h, r * GRID_ROW_PITCH:r * GRID_ROW_PITCH + GRID_W, :]
                                    for r in range(grid_rows)], axis=0) for h in range(GLA_HEADS)]
    else:
        o_sum = of_ref[...].astype(F32) + ob_ref[...].astype(F32)
        o_heads = [o_sum[:, h * GLA_DV:(h + 1) * GLA_DV] for h in range(GLA_HEADS)]
    zg = zg_ref[...].astype(F32)
    gate_g = zg * jax.nn.sigmoid(zg)
    for h in range(GLA_HEADS):
        o_h = o_heads[h]
        o_n = o_h * lax.rsqrt(jnp.mean(o_h * o_h, axis=-1, keepdims=True) + EPS) * gn_ref[...]
        a_h = (o_n * gate_g[:, h * GLA_DV:(h + 1) * GLA_DV]).astype(BF16)
        y = y + jnp.dot(a_h, wout_ref[S5_WIDTH + h * GLA_DV:S5_WIDTH + (h + 1) * GLA_DV, :],
                        preferred_element_type=F32)
    gate = mod_ref[:, 2 * D_MODEL:3 * D_MODEL]
    xn = x_ref[...] + gate * y
    if final:
        ms = jnp.mean(xn * xn, axis=-1, keepdims=True)
        xn = xn * lax.rsqrt(ms + EPS) * fin_ref[...]
    o_ref[...] = xn


def _finish(ys, zs, o_f, o_b, zg, x2, mods, ctx_mod_row, w_glu, b_glu, gnorm, w_out, fin_gain, final, colmajor):
    rows = x2.shape[0]
    per_seq = rows // ys.shape[0]
    tb = min(FINISH_BLOCK, per_seq)
    bps = per_seq // tb
    mod_row = (lambda i: i // bps) if ctx_mod_row is None else (lambda i: ctx_mod_row)
    rb = lambda w: pl.BlockSpec((tb, w), lambda i: (i, 0))
    full = lambda a: pl.BlockSpec(a.shape, lambda i: (0,) * a.ndim)
    if colmajor:
        assert tb % GRID_W == 0
        o_spec = pl.BlockSpec((None, GRID_W, tb // GRID_W, GLA_WIDTH), lambda i: (i // bps, 0, i % bps, 0))
    else:
        o_spec = rb(GLA_WIDTH)
    return pl.pallas_call(
        functools.partial(_finish_kernel, final=final, colmajor=colmajor),
        grid=(rows // tb,),
        in_specs=[pl.BlockSpec((None, S5_GROUPS, tb // S5_CHUNK, S5_ROW), lambda i: (i // bps, 0, i % bps, 0)),
                  rb(S5_WIDTH), o_spec, o_spec, rb(GLA_WIDTH), rb(D_MODEL),
                  pl.BlockSpec((None, 1, 3 * D_MODEL), lambda i: (mod_row(i), 0, 0)),
                  full(w_glu), full(b_glu), full(gnorm), full(w_out), full(fin_gain)],
        out_specs=rb(D_MODEL),
        out_shape=jax.ShapeDtypeStruct((rows, D_MODEL), F32),
        scratch_shapes=[pltpu.VMEM((S5_WIDTH // LANES, tb // S5_CHUNK * CHUNK_PITCH, LANES), F32),
                        pltpu.VMEM((GLA_HEADS, max(tb // GRID_W, 1) * GRID_ROW_PITCH, GLA_DV), F32)],
        compiler_params=_cparams("arbitrary"),
        name="finish",
    )(ys, zs, o_f, o_b, zg, x2, mods, w_glu, b_glu, gnorm, w_out, fin_gain)


def _layer(x_lat, x_ctx, mods, lw, need_ctx_out, final):
    bsz, l, _ = x_lat.shape
    lc = x_ctx.shape[1]
    lat_row = None
    ctx_row = bsz

    pj_l = _inproj(x_lat.reshape(bsz * l, D_MODEL), mods, lw["gain"], lw["w_in"], lat_row, bsz, colmajor=True)
    pj_c = _inproj(x_ctx.reshape(bsz * lc, D_MODEL), mods, lw["gain"], lw["w_in"], ctx_row, bsz, colmajor=False)
    xg_l, zs_l, q_l, k_l, v_l, zg_l, lr_l = pj_l
    xg_c, zs_c, q_c, k_c, v_c, zg_c, lr_c = pj_c

    ys_c, ys_l = _s5_branch(xg_c, xg_l, lw["s5"], need_ctx_out)

    cols = lambda t: t.reshape(bsz, lc // GLA_COL, GLA_COL, t.shape[-1])
    s_zero = jnp.zeros((bsz, 2, GLA_KEY, GLA_DV), F32)
    ocf, ocb, s_ctx = _gla(cols(q_c), cols(k_c), cols(v_c), cols(lr_c), lw["wg"], lw["bg"], s_zero)
    olf, olb, _ = _gla(q_l, k_l, v_l, lr_l, lw["wg"], lw["bg"], s_ctx)

    fin_args = (lw["w_glu"], lw["b_glu"], lw["gnorm"], lw["w_out"], lw["fin"])
    x_lat_new = _finish(ys_l, zs_l, olf, olb, zg_l, x_lat.reshape(bsz * l, D_MODEL), mods, lat_row,
                        *fin_args, final=final, colmajor=True).reshape(bsz, l, D_MODEL)
    x_ctx_new = x_ctx
    if need_ctx_out:
        x_ctx_new = _finish(ys_c, zs_c, ocf.reshape(bsz * lc, GLA_WIDTH), ocb.reshape(bsz * lc, GLA_WIDTH), zg_c,
                            x_ctx.reshape(bsz * lc, D_MODEL), mods, ctx_row, *fin_args, final=False,
                            colmajor=False).reshape(bsz, lc, D_MODEL)
    return x_lat_new, x_ctx_new


def _stack_weights(norm_g, w_in, s5_lam_re, s5_lam_im, s5_log_dt, s5_b_re, s5_b_im, s5_c_re, s5_c_im, s5_d,
                   s5_w_glu, s5_b_glu, gla_w_gate, gla_b_gate, gla_norm_g, w_out, final_norm):
    w_pad = jnp.pad(w_in, ((0, 0), (0, 0), (0, IN_PAD - w_in.shape[-1]))).astype(BF16)
    wg = jnp.zeros((DEPTH, 2, LR_PAD, GLA_KEY), F32)
    for d in range(2):
        wg = wg.at[:, d, d * GLA_RANK:(d + 1) * GLA_RANK].set(gla_w_gate[:, d])
    lag_rows, out_rows, inj_rows, a1, a2 = jax.vmap(_s5_matrices)(
        s5_lam_re, s5_lam_im, s5_log_dt, s5_b_re, s5_b_im, s5_c_re, s5_c_im, s5_d)
    merged = lambda t: t.reshape((DEPTH * S5_GROUPS,) + t.shape[2:])
    toep_rows = _s5_lags(merged(lag_rows), merged(out_rows)).reshape(DEPTH, S5_GROUPS, 2, S5_GROUP, 2 * S5_ROW)
    s5 = (toep_rows, out_rows, inj_rows, a1, a2)
    wg = wg.astype(BF16)
    w_glu = s5_w_glu.astype(BF16)
    w_o = w_out.astype(BF16)
    return [{
        "gain": norm_g[i].reshape(1, D_MODEL),
        "w_in": w_pad[i],
        "s5": tuple(m[i] for m in s5),
        "wg": wg[i],
        "bg": gla_b_gate[i].reshape(2, 1, GLA_KEY),
        "w_glu": w_glu[i],
        "b_glu": s5_b_glu[i].reshape(1, S5_WIDTH),
        "gnorm": gla_norm_g[i].reshape(1, GLA_DV),
        "w_out": w_o[i],
        "fin": final_norm.reshape(1, D_MODEL),
    } for i in range(DEPTH)]


def kernel(x, c, ctx, c_ctx, norm_g, w_mod, b_mod, w_in, s5_lam_re, s5_lam_im, s5_log_dt, s5_b_re, s5_b_im,
           s5_c_re, s5_c_im, s5_d, s5_w_glu, s5_b_glu, gla_w_gate, gla_b_gate, gla_norm_g, w_out, final_norm):
    bsz = x.shape[0]
    cond8 = jnp.concatenate([c, c_ctx[None], jnp.zeros((8 - bsz - 1, D_MODEL), F32)], axis=0)
    mods_all = _modulation(cond8, w_mod, b_mod)
    weights = _stack_weights(norm_g, w_in, s5_lam_re, s5_lam_im, s5_log_dt, s5_b_re, s5_b_im, s5_c_re, s5_c_im,
                             s5_d, s5_w_glu, s5_b_glu, gla_w_gate, gla_b_gate, gla_norm_g, w_out, final_norm)
    x_lat, x_ctx = x, ctx
    for i in range(DEPTH):
        mods = mods_all[i].reshape(8, 1, 3 * D_MODEL)
        last = i == DEPTH - 1
        x_lat, x_ctx = _layer(x_lat, x_ctx, mods, weights[i], need_ctx_out=not last, final=last)
    return x_lat
```

```python
import functools

import jax
import jax.numpy as jnp
from jax import lax
from jax.experimental import pallas as pl
from jax.experimental.pallas import tpu as pltpu

F32 = jnp.float32
BF16 = jnp.bfloat16

D_MODEL = 1024
DEPTH = 4
GRID_W = 64
EPS = 1e-6
LANES = 128
SUBLANES = 8
S5_WIDTH = 512
S5_GROUP = 16
S5_GROUPS = 32
S5_STATE = 64
S5_CHUNK = 16
S5_ROW = S5_CHUNK * S5_GROUP
S5_HALF = 2 * S5_STATE
S5_SLANES = 2 * S5_HALF
S5_SCAN_BLOCK = 32
S5_GROUP_BLOCK = 8
SLOTS = LANES // S5_GROUP
GLA_HEADS = 4
GLA_DV = 128
GLA_DK = 64
GLA_WIDTH = 512
GLA_KEY = 256
GLA_RANK = 16
GLA_NORMALIZER = 16.0
GLA_CHUNK = 64
GLA_COL = 128
LR_PAD = 128
IN_PAD = 2 * S5_WIDTH + 2 * GLA_KEY + 2 * GLA_WIDTH + LR_PAD

ROW_BLOCK = 512
GRID_ROWS_PER_BLOCK = ROW_BLOCK // GRID_W
CHUNK_PITCH = S5_CHUNK + SUBLANES
GRID_ROW_PITCH = GRID_W + SUBLANES
VMEM_LIMIT = 48 * 1024 * 1024


def _cparams(*sem):
    return pltpu.CompilerParams(dimension_semantics=sem, vmem_limit_bytes=VMEM_LIMIT)


def _mod_kernel(cond_ref, w_ref, b_ref, o_ref):
    cnd = cond_ref[...]
    a = cnd * jax.nn.sigmoid(cnd)
    o_ref[...] = jnp.dot(a, w_ref[...], preferred_element_type=F32,
                         precision=lax.Precision.HIGHEST) + b_ref[...]


def _modulation(cond8, w_mod, b_mod):
    nblk = 3
    return pl.pallas_call(
        _mod_kernel,
        grid=(DEPTH, nblk),
        in_specs=[
            pl.BlockSpec((8, D_MODEL), lambda l, j: (0, 0)),
            pl.BlockSpec((None, D_MODEL, D_MODEL), lambda l, j: (l, 0, j)),
            pl.BlockSpec((None, 1, D_MODEL), lambda l, j: (l, 0, j)),
        ],
        out_specs=pl.BlockSpec((None, 8, D_MODEL), lambda l, j: (l, 0, j)),
        out_shape=jax.ShapeDtypeStruct((DEPTH, 8, 3 * D_MODEL), F32),
        compiler_params=_cparams("arbitrary", "arbitrary"),
        name="modulation",
    )(cond8, w_mod, b_mod.reshape(DEPTH, 1, 3 * D_MODEL))


_IN_SEGS = (("zs", S5_WIDTH, False), ("q", GLA_KEY, True), ("k", GLA_KEY, True), ("v", GLA_WIDTH, True),
            ("zg", GLA_WIDTH, False), ("lr", LR_PAD, True))
_BF16_SEGS = ("zs", "zg")


def _swap_slots_with_index(vs):
    lane = lax.broadcasted_iota(jnp.int32, vs[0].shape, 1)
    for d in (4, 2, 1):
        sh = d * S5_GROUP
        low = (lane & sh) == 0
        nxt = list(vs)
        for i in range(SLOTS):
            if i & d == 0:
                a, b = vs[i], vs[i + d]
                nxt[i] = jnp.where(low, a, pltpu.roll(b, sh, 1))
                nxt[i + d] = jnp.where(low, pltpu.roll(a, LANES - sh, 1), b)
        vs = nxt
    return vs


def _inproj_kernel(x_ref, mod_ref, gain_ref, w_ref, xg_ref, *rest, colmajor):
    out_refs, u_scr, c_scr = rest[:-2], rest[-2], rest[-1]
    x = x_ref[...]
    ms = jnp.mean(x * x, axis=-1, keepdims=True)
    y = x * lax.rsqrt(ms + EPS) * gain_ref[...]
    shift = mod_ref[:, 0:D_MODEL]
    scale = mod_ref[:, D_MODEL:2 * D_MODEL]
    h = (y * (1.0 + scale) + shift).astype(BF16)
    u = jnp.dot(h, w_ref[:, 0:S5_WIDTH], preferred_element_type=F32)
    nchunk = u.shape[0] // S5_CHUNK
    for kt in range(S5_WIDTH // LANES):
        for c in range(nchunk):
            u_scr[kt, c * CHUNK_PITCH:c * CHUNK_PITCH + S5_CHUNK, :] = u[c * S5_CHUNK:(c + 1) * S5_CHUNK,
                                                                         kt * LANES:(kt + 1) * LANES]
    off = S5_WIDTH
    for (_, width, cm), o_ref in zip(_IN_SEGS, out_refs):
        val = jnp.dot(h, w_ref[:, off:off + width], preferred_element_type=F32)
        off += width
        if colmajor and cm:
            for kt in range(width // LANES):
                for r in range(GRID_ROWS_PER_BLOCK):
                    c_scr[kt, r * GRID_ROW_PITCH:r * GRID_ROW_PITCH + GRID_W, :] = val[r * GRID_W:(r + 1) * GRID_W,
                                                                                       kt * LANES:(kt + 1) * LANES]
            for w in range(GRID_W):
                for kt in range(width // LANES):
                    o_ref[w, :, kt * LANES:(kt + 1) * LANES] = c_scr[kt, pl.ds(w, GRID_ROWS_PER_BLOCK,
                                                                              stride=GRID_ROW_PITCH), :]
        else:
            o_ref[...] = val.astype(o_ref.dtype)
    for half in range(S5_CHUNK // SLOTS):
        for kt in range(S5_WIDTH // LANES):
            vs = [u_scr[kt, pl.ds(half * SLOTS + jj, nchunk, stride=CHUNK_PITCH), :] for jj in range(SLOTS)]
            ws = _swap_slots_with_index(vs)
            for qq in range(SLOTS):
                xg_ref[kt * SLOTS + qq, :, half * LANES:(half + 1) * LANES] = ws[qq].astype(BF16)


def _inproj(x2, mods, gain, w_pad, layer, ctx_mod_row, bsz, colmajor):
    rows = x2.shape[0]
    per_seq = rows // bsz
    tb = min(ROW_BLOCK, per_seq)
    bps = per_seq // tb
    mod_row = (lambda i: layer * 8 + i // bps) if ctx_mod_row is None else (lambda i: layer * 8 + ctx_mod_row)
    out_specs = [pl.BlockSpec((None, S5_GROUPS, tb // S5_CHUNK, S5_ROW), lambda i: (i // bps, 0, i % bps, 0))]
    out_shape = [jax.ShapeDtypeStruct((bsz, S5_GROUPS, per_seq // S5_CHUNK, S5_ROW), BF16)]
    for name, w, cm in _IN_SEGS:
        if colmajor and cm:
            assert tb == ROW_BLOCK
            out_specs.append(pl.BlockSpec((None, GRID_W, GRID_ROWS_PER_BLOCK, w),
                                          lambda i: (i // bps, 0, i % bps, 0)))
            out_shape.append(jax.ShapeDtypeStruct((bsz, GRID_W, per_seq // GRID_W, w), F32))
        else:
            out_specs.append(pl.BlockSpec((tb, w), lambda i: (i, 0)))
            out_shape.append(jax.ShapeDtypeStruct((rows, w), BF16 if name in _BF16_SEGS else F32))
    return pl.pallas_call(
        functools.partial(_inproj_kernel, colmajor=colmajor),
        grid=(rows // tb,),
        in_specs=[
            pl.BlockSpec((tb, D_MODEL), lambda i: (i, 0)),
            pl.BlockSpec((None, 1, 3 * D_MODEL), lambda i: (mod_row(i), 0, 0)),
            pl.BlockSpec((None, 1, D_MODEL), lambda i: (layer, 0, 0)),
            pl.BlockSpec((None, D_MODEL, IN_PAD), lambda i: (layer, 0, 0)),
        ],
        out_specs=out_specs,
        out_shape=out_shape,
        scratch_shapes=[pltpu.VMEM((S5_WIDTH // LANES, tb // S5_CHUNK * CHUNK_PITCH, LANES), F32),
                        pltpu.VMEM((GLA_WIDTH // LANES, GRID_ROWS_PER_BLOCK * GRID_ROW_PITCH, LANES), F32)],
        compiler_params=_cparams("arbitrary"),
        name="inproj",
    )(x2, mods, gain, w_pad)


def _s5_matrices(lam_re, lam_im, log_dt, b_re, b_im, c_re, c_im, d_skip):
    T = S5_CHUNK
    lam = lax.complex(lam_re.astype(F32), lam_im.astype(F32))
    dt = jnp.exp(log_dt.astype(F32))[..., None]
    ldt = lam * dt
    lam_bar = jnp.exp(ldt)
    b_c = lax.complex(b_re.astype(F32), b_im.astype(F32))
    b_bar = ((lam_bar - 1.0) / lam)[..., None] * b_c[None]
    c_c = lax.complex(c_re.astype(F32), c_im.astype(F32))
    steps = jnp.arange(T + 1, dtype=F32)
    pw = jnp.exp(ldt[:, None] * steps[None, :, None, None].astype(jnp.complex64))

    lanes = lambda parts: jnp.concatenate(parts, axis=-1)
    by_group = lambda p: p.transpose(1, 0, 2)

    qf = by_group(pw[0][:T][::-1])
    qb = by_group(pw[1][:T])
    bt = b_bar.transpose(0, 1, 3, 2)
    inj_rows = jnp.stack([lanes([qf.real, qf.real, qb.real, qb.real]),
                          lanes([-qf.imag, qf.imag, -qb.imag, qb.imag]),
                          lanes([bt[0].real, bt[0].imag, bt[1].real, bt[1].imag]),
                          lanes([bt[0].imag, bt[0].real, bt[1].imag, bt[1].real])], axis=1)

    pf = by_group(pw[0][1:T + 1])
    pb = by_group(pw[1][::-1][:T])
    out_rows = jnp.stack([lanes([pf.real, -pf.imag, pb.real, -pb.imag]),
                          lanes([-pf.imag, -pf.real, -pb.imag, -pb.real]),
                          lanes([c_c.real] * 4),
                          lanes([c_c.imag] * 4)], axis=1)

    kf_pw = by_group(pw[0][:T])
    kb_pw = by_group(pw[1][:T][::-1])
    skip = jnp.eye(S5_GROUP, dtype=F32)[None] * d_skip.astype(F32).reshape(S5_GROUPS, 1, S5_GROUP)
    lag_rows = jnp.stack([lanes([kf_pw.real, -kf_pw.imag, kb_pw.real, -kb_pw.imag]),
                          lanes([-kf_pw.imag, -kf_pw.real, -kb_pw.imag, -kb_pw.real]),
                          inj_rows[:, 2],
                          jnp.pad(skip, ((0, 0), (0, 0), (0, S5_ROW - S5_GROUP)))], axis=1)

    a_chunk = pw[:, T]
    a1 = jnp.concatenate([a_chunk.real, a_chunk.real], axis=-1)
    a2 = jnp.concatenate([-a_chunk.imag, a_chunk.imag], axis=-1)
    return lag_rows, out_rows, inj_rows, a1, a2


def _factor_rows_to_matrix(r_ref, gi, dst):
    s_a = r_ref[gi, 2]
    s_b = r_ref[gi, 3]
    for a in range(S5_CHUNK):
        rows = r_ref[gi, 0, a:a + 1, :] * s_a + r_ref[gi, 1, a:a + 1, :] * s_b
        dst[gi, a * S5_GROUP:(a + 1) * S5_GROUP, :] = rows.astype(BF16)


def _s5_inject_kernel(x_ref, r_ref, sf_ref, sb_ref, m_scr):
    for gi in range(S5_GROUP_BLOCK):
        _factor_rows_to_matrix(r_ref, gi, m_scr)
        s = jnp.dot(x_ref[gi], m_scr[gi], preferred_element_type=F32)
        sf_ref[gi] = s[:, 0:S5_HALF]
        sb_ref[gi] = s[:, S5_HALF:S5_SLANES]


def _s5_inject(xg, inj_rows, layer):
    bsz, _, nchunk, _ = xg.shape
    gb = S5_GROUP_BLOCK
    g0 = layer * (S5_GROUPS // gb)
    s_spec = pl.BlockSpec((None, gb, nchunk, S5_HALF), lambda b, g: (b, g, 0, 0))
    s_shape = jax.ShapeDtypeStruct((bsz, S5_GROUPS, nchunk, S5_HALF), F32)
    return pl.pallas_call(
        _s5_inject_kernel,
        grid=(bsz, S5_GROUPS // gb),
        in_specs=[
            pl.BlockSpec((None, gb, nchunk, S5_ROW), lambda b, g: (b, g, 0, 0)),
            pl.BlockSpec((gb, 4, S5_GROUP, S5_SLANES), lambda b, g: (g0 + g, 0, 0, 0)),
        ],
        out_specs=[s_spec, s_spec],
        out_shape=[s_shape, s_shape],
        scratch_shapes=[pltpu.VMEM((gb, S5_ROW, S5_SLANES), BF16)],
        compiler_params=_cparams("arbitrary", "arbitrary"),
        name="s5_inject",
    )(xg, inj_rows)


def _s5_scan_kernel(sf_ref, sb_ref, a1_ref, a2_ref, h0_ref, hf_ref, hb_ref, hfin_ref, hf_scr, hb_scr, *, cpb):
    @pl.when(pl.program_id(0) == 0)
    def _():
        hf_scr[...] = h0_ref[0]
        hb_scr[...] = h0_ref[1]

    swap = lambda t: pltpu.roll(t, S5_STATE, 1)
    a1f = a1_ref[0]
    a2f = a2_ref[0]
    a1b = a1_ref[1]
    a2b = a2_ref[1]
    hf = hf_scr[...]
    hb = hb_scr[...]
    hf_sw = swap(hf)
    hb_sw = swap(hb)
    for c in range(cpb):
        hf_ref[:, c * S5_HALF:(c + 1) * S5_HALF] = hf.astype(BF16)
        s_f = sf_ref[:, c * S5_HALF:(c + 1) * S5_HALF]
        hf, hf_sw = a1f * hf + a2f * hf_sw + s_f, a1f * hf_sw - a2f * hf + swap(s_f)
        cb = cpb - 1 - c
        hb_ref[:, cb * S5_HALF:(cb + 1) * S5_HALF] = hb.astype(BF16)
        s_b = sb_ref[:, cb * S5_HALF:(cb + 1) * S5_HALF]
        hb, hb_sw = a1b * hb + a2b * hb_sw + s_b, a1b * hb_sw - a2b * hb + swap(s_b)
    hf_scr[...] = hf
    hb_scr[...] = hb

    @pl.when(pl.program_id(0) == pl.num_programs(0) - 1)
    def _():
        hfin_ref[0] = hf
        hfin_ref[1] = hb


def _s5_scan(s_f, s_b, a1, a2, h0):
    bsz, _, nchunk, _ = s_f.shape
    rows = bsz * S5_GROUPS
    cpb = min(S5_SCAN_BLOCK, nchunk)
    nblk = nchunk // cpb
    fwd = pl.BlockSpec((rows, cpb * S5_HALF), lambda i: (0, i))
    bwd = pl.BlockSpec((rows, cpb * S5_HALF), lambda i: (0, nblk - 1 - i))
    small = pl.BlockSpec((2, rows, S5_HALF), lambda i: (0, 0, 0))
    hf, hb, hfin = pl.pallas_call(
        functools.partial(_s5_scan_kernel, cpb=cpb),
        grid=(nblk,),
        in_specs=[fwd, bwd, small, small, small],
        out_specs=[fwd, bwd, small],
        out_shape=[jax.ShapeDtypeStruct((rows, nchunk * S5_HALF), BF16)] * 2
        + [jax.ShapeDtypeStruct((2, rows, S5_HALF), F32)],
        scratch_shapes=[pltpu.VMEM((rows, S5_HALF), F32), pltpu.VMEM((rows, S5_HALF), F32)],
        compiler_params=_cparams("arbitrary"),
        name="s5_scan",
    )(s_f.reshape(rows, nchunk * S5_HALF), s_b.reshape(rows, nchunk * S5_HALF), a1, a2, h0)
    shape4 = (bsz, S5_GROUPS, nchunk, S5_HALF)
    return hf.reshape(shape4), hb.reshape(shape4), hfin


def _dot_nt(a, b):
    return lax.dot_general(a, b, (((1,), (1,)), ((), ())), preferred_element_type=F32)


def _s5_lag_kernel(t_ref, r_ref, o_ref):
    hp = lax.Precision.HIGHEST
    nt = (((1,), (1,)), ((), ()))
    zero_rows = jnp.zeros((S5_GROUP, S5_ROW), F32)
    for gi in range(S5_GROUP_BLOCK):
        c_re = r_ref[gi, 2]
        c_im = r_ref[gi, 3]
        c_pw = jnp.concatenate([t_ref[gi, 0, t:t + 1, :] * c_re + t_ref[gi, 1, t:t + 1, :] * c_im
                                for t in range(S5_CHUNK)], axis=0)
        b_bar = t_ref[gi, 2]
        lag_f = lax.dot_general(b_bar[:, 0:S5_HALF], c_pw[:, 0:S5_HALF], nt, precision=hp,
                                preferred_element_type=F32) + t_ref[gi, 3]
        lag_b = lax.dot_general(b_bar[:, S5_HALF:S5_SLANES], c_pw[:, S5_HALF:S5_SLANES], nt, precision=hp,
                                preferred_element_type=F32)
        o_ref[gi, 0] = jnp.concatenate([zero_rows, lag_f], axis=1)
        o_ref[gi, 1] = jnp.concatenate([lag_b, zero_rows], axis=1)


def _s5_lags(lag_rows, out_rows):
    n = lag_rows.shape[0]
    gb = S5_GROUP_BLOCK
    spec = pl.BlockSpec((gb, 4, S5_GROUP, S5_SLANES), lambda g: (g, 0, 0, 0))
    return pl.pallas_call(
        _s5_lag_kernel,
        grid=(n // gb,),
        in_specs=[spec, spec],
        out_specs=pl.BlockSpec((gb, 2, S5_GROUP, 2 * S5_ROW), lambda g: (g, 0, 0, 0)),
        out_shape=jax.ShapeDtypeStruct((n, 2, S5_GROUP, 2 * S5_ROW), F32),
        compiler_params=_cparams("arbitrary"),
        name="s5_lags",
    )(lag_rows, out_rows)


def _s5_output_kernel(x_ref, hf_ref, hb_ref, t_ref, r_ref, y_ref, m_scr, c_scr):
    for gi in range(S5_GROUP_BLOCK):
        k_f = t_ref[gi, 0]
        k_b = t_ref[gi, 1]
        for i in range(S5_CHUNK):
            lo_f = S5_ROW - i * S5_GROUP
            lo_b = (S5_CHUNK - 1 - i) * S5_GROUP
            m_scr[gi, i * S5_GROUP:(i + 1) * S5_GROUP, :] = (k_f[:, lo_f:lo_f + S5_ROW]
                                                             + k_b[:, lo_b:lo_b + S5_ROW]).astype(BF16)
        _factor_rows_to_matrix(r_ref, gi, c_scr)
        y = jnp.dot(x_ref[gi], m_scr[gi], preferred_element_type=F32)
        y = y + _dot_nt(hf_ref[gi], c_scr[gi, :, 0:S5_HALF])
        y = y + _dot_nt(hb_ref[gi], c_scr[gi, :, S5_HALF:S5_SLANES])
        y_ref[gi] = y.astype(BF16)


def _s5_output(xg, hf, hb, toep_rows, out_rows, layer):
    bsz, _, nchunk, _ = xg.shape
    gb = S5_GROUP_BLOCK
    g0 = layer * (S5_GROUPS // gb)
    return pl.pallas_call(
        _s5_output_kernel,
        grid=(bsz, S5_GROUPS // gb),
        in_specs=[
            pl.BlockSpec((None, gb, nchunk, S5_ROW), lambda b, g: (b, g, 0, 0)),
            pl.BlockSpec((None, gb, nchunk, S5_HALF), lambda b, g: (b, g, 0, 0)),
            pl.BlockSpec((None, gb, nchunk, S5_HALF), lambda b, g: (b, g, 0, 0)),
            pl.BlockSpec((gb, 2, S5_GROUP, 2 * S5_ROW), lambda b, g: (g0 + g, 0, 0, 0)),
            pl.BlockSpec((gb, 4, S5_GROUP, S5_SLANES), lambda b, g: (g0 + g, 0, 0, 0)),
        ],
        out_specs=pl.BlockSpec((None, gb, nchunk, S5_ROW), lambda b, g: (b, g, 0, 0)),
        out_shape=jax.ShapeDtypeStruct((bsz, S5_GROUPS, nchunk, S5_ROW), BF16),
        scratch_shapes=[pltpu.VMEM((gb, S5_ROW, S5_ROW), BF16), pltpu.VMEM((gb, S5_ROW, S5_SLANES), BF16)],
        compiler_params=_cparams("arbitrary", "arbitrary"),
        name="s5_output",
    )(xg, hf, hb, toep_rows, out_rows)


def _s5_branch(xg_c, xg_l, mats, layer, need_ctx):
    toep_rows, out_rows, inj_rows, a1, a2 = mats
    bsz = xg_l.shape[0]
    a1r = jnp.tile(a1[layer], (1, bsz, 1))
    a2r = jnp.tile(a2[layer], (1, bsz, 1))
    sf_c, sb_c = _s5_inject(xg_c, inj_rows, layer)
    sf_l, sb_l = _s5_inject(xg_l, inj_rows, layer)
    hf_c, hb_c, h_ctx = _s5_scan(sf_c, sb_c, a1r, a2r, jnp.zeros_like(a1r))
    hf_l, hb_l, _ = _s5_scan(sf_l, sb_l, a1r, a2r, h_ctx)
    y_l = _s5_output(xg_l, hf_l, hb_l, toep_rows, out_rows, layer)
    y_c = _s5_output(xg_c, hf_c, hb_c, toep_rows, out_rows, layer) if need_ctx else None
    return y_c, y_l


def _gla_kernel(qf_ref, kf_ref, vf_ref, lrf_ref, qb_ref, kb_ref, vb_ref, lrb_ref, wg_ref, bg_ref, s0_ref,
                of_ref, ob_ref, sfin_ref, s_scr):
    w = pl.program_id(0)

    @pl.when(w == 0)
    def _():
        s_scr[...] = s0_ref[...]

    row = lax.broadcasted_iota(jnp.int32, (GLA_COL, GLA_COL), 0)
    col = lax.broadcasted_iota(jnp.int32, (GLA_COL, GLA_COL), 1)
    same_chunk = (row // GLA_CHUNK) == (col // GLA_CHUNK)
    r64 = lax.broadcasted_iota(jnp.int32, (GLA_CHUNK, GLA_CHUNK), 0)
    c64 = lax.broadcasted_iota(jnp.int32, (GLA_CHUNK, GLA_CHUNK), 1)
    dirs = (
        (qf_ref, kf_ref, vf_ref, lrf_ref, of_ref, jnp.where(same_chunk & (col <= row), 1.0, 0.0).astype(BF16),
         c64 <= r64),
        (qb_ref, kb_ref, vb_ref, lrb_ref, ob_ref, jnp.where(same_chunk & (col >= row), 1.0, 0.0).astype(BF16),
         c64 >= r64),
    )
    chains = [(b, d) for b in range(s_scr.shape[0]) for d in range(2)]

    pre = []
    for b, d in chains:
        q_ref, k_ref, _, lr_ref, _, tri, _ = dirs[d]
        z = jnp.dot(lr_ref[b].astype(BF16), wg_ref[d], preferred_element_type=F32) + bg_ref[d]
        g = (jnp.minimum(z, 0.0) - jnp.log1p(jnp.exp(-jnp.abs(z)))) * (1.0 / GLA_NORMALIZER)
        g_hi = g.astype(BF16)
        g_lo = (g - g_hi.astype(F32)).astype(BF16)
        gc = jnp.dot(tri, g_hi, preferred_element_type=F32) + jnp.dot(tri, g_lo, preferred_element_type=F32)
        qt = (q_ref[b] * jnp.exp(gc) * (GLA_DK ** -0.5)).astype(BF16)
        pre.append((qt, gc.T, k_ref[b].T))

    for pos in range(2):
        keys = []
        for (b, d), (qt, gct, kt) in zip(chains, pre):
            ci = pos if d == 0 else 1 - pos
            r0 = ci * GLA_CHUNK
            last = r0 + GLA_CHUNK - 1 if d == 0 else r0
            gct_c = gct[:, r0:r0 + GLA_CHUNK]
            dec = jnp.exp(gct[:, last:last + 1])
            k_in = kt[:, r0:r0 + GLA_CHUNK] * jnp.exp(-gct_c)
            keys.append((r0, dec, k_in.astype(BF16), (k_in * dec).astype(BF16)))
        for h in range(GLA_HEADS):
            ks = slice(h * GLA_DK, (h + 1) * GLA_DK)
            vs = slice(h * GLA_DV, (h + 1) * GLA_DV)
            scores = []
            for (b, d), (qt, _, _), (r0, _, k_in, _) in zip(chains, pre, keys):
                sc = jnp.dot(qt[r0:r0 + GLA_CHUNK, ks], k_in[ks, :], preferred_element_type=F32)
                scores.append(jnp.where(dirs[d][6], sc, 0.0).astype(BF16))
            for (b, d), (qt, _, _), (r0, dec, _, k_st), sc in zip(chains, pre, keys, scores):
                v_h = dirs[d][2][b, r0:r0 + GLA_CHUNK, vs].astype(BF16)
                s_h = s_scr[b, d, ks, :]
                o_h = jnp.dot(sc, v_h, preferred_element_type=F32)
                o_h = o_h + jnp.dot(qt[r0:r0 + GLA_CHUNK, ks], s_h.astype(BF16), preferred_element_type=F32)
                dirs[d][4][b, r0:r0 + GLA_CHUNK, vs] = o_h
                s_scr[b, d, ks, :] = dec[ks, :] * s_h + jnp.dot(k_st[ks, :], v_h, preferred_element_type=F32)

    @pl.when(w == pl.num_programs(0) - 1)
    def _():
        sfin_ref[...] = s_scr[...]


def _gla(q, k, v, lr, wg, bg, layer, s0):
    bsz, ncol = q.shape[:2]
    fwd = lambda w: (0, w, 0, 0)
    bwd = lambda w: (0, ncol - 1 - w, 0, 0)
    widths = (GLA_KEY, GLA_KEY, GLA_WIDTH, LR_PAD)
    state_spec = pl.BlockSpec((bsz, 2, GLA_KEY, GLA_DV), lambda w: (0, 0, 0, 0))
    in_specs = ([pl.BlockSpec((bsz, None, GLA_COL, wd), fwd) for wd in widths]
                + [pl.BlockSpec((bsz, None, GLA_COL, wd), bwd) for wd in widths]
                + [pl.BlockSpec((None, 2, LR_PAD, GLA_KEY), lambda w: (layer, 0, 0, 0)),
                   pl.BlockSpec((None, 2, 1, GLA_KEY), lambda w: (layer, 0, 0, 0)),
                   state_spec])
    o_shape = jax.ShapeDtypeStruct((bsz, ncol, GLA_COL, GLA_WIDTH), F32)
    return pl.pallas_call(
        _gla_kernel,
        grid=(ncol,),
        in_specs=in_specs,
        out_specs=[pl.BlockSpec((bsz, None, GLA_COL, GLA_WIDTH), fwd),
                   pl.BlockSpec((bsz, None, GLA_COL, GLA_WIDTH), bwd),
                   state_spec],
        out_shape=[o_shape, o_shape, jax.ShapeDtypeStruct((bsz, 2, GLA_KEY, GLA_DV), F32)],
        scratch_shapes=[pltpu.VMEM((bsz, 2, GLA_KEY, GLA_DV), F32)],
        compiler_params=_cparams("arbitrary"),
        name="gla",
    )(q, k, v, lr, q, k, v, lr, wg, bg, s0)


def _finish_kernel(ys_ref, zs_ref, of_ref, ob_ref, zg_ref, x_ref, mod_ref, wglu_ref, bglu_ref, gn_ref, wout_ref,
                   fin_ref, o_ref, ys_scr, o_scr, *, final, colmajor):
    nchunk = ys_ref.shape[1]
    for half in range(S5_CHUNK // SLOTS):
        for kt in range(S5_WIDTH // LANES):
            ws = [ys_ref[kt * SLOTS + qq, :, half * LANES:(half + 1) * LANES].astype(F32) for qq in range(SLOTS)]
            vs = _swap_slots_with_index(ws)
            for jj in range(SLOTS):
                ys_scr[kt, pl.ds(half * SLOTS + jj, nchunk, stride=CHUNK_PITCH), :] = vs[jj]
    ys = jnp.concatenate(
        [jnp.concatenate([ys_scr[kt, c * CHUNK_PITCH:c * CHUNK_PITCH + S5_CHUNK, :] for c in range(nchunk)], axis=0)
         for kt in range(S5_WIDTH // LANES)], axis=1)
    s = jax.nn.gelu(ys, approximate=True)
    t = jnp.dot(s.astype(BF16), wglu_ref[...], preferred_element_type=F32) + bglu_ref[...]
    zs = zs_ref[...].astype(F32)
    a_s = (s * jax.nn.sigmoid(t)) * (zs * jax.nn.sigmoid(zs))
    y = jnp.dot(a_s.astype(BF16), wout_ref[0:S5_WIDTH, :], preferred_element_type=F32)

    if colmajor:
        grid_rows = of_ref.shape[1]
        for w in range(GRID_W):
            o_w = of_ref[w] + ob_ref[w]
            for h in range(GLA_HEADS):
                o_scr[h, pl.ds(w, grid_rows, stride=GRID_ROW_PITCH), :] = o_w[:, h * GLA_DV:(h + 1) * GLA_DV]
        o_heads = [jnp.concatenate([o_scr[h, r * GRID_ROW_PITCH:r * GRID_ROW_PITCH + GRID_W, :]
                                    for r in range(grid_rows)], axis=0) for h in range(GLA_HEADS)]
    else:
        o_sum = of_ref[...] + ob_ref[...]
        o_heads = [o_sum[:, h * GLA_DV:(h + 1) * GLA_DV] for h in range(GLA_HEADS)]
    zg = zg_ref[...].astype(F32)
    gate_g = zg * jax.nn.sigmoid(zg)
    for h in range(GLA_HEADS):
        o_h = o_heads[h]
        o_n = o_h * lax.rsqrt(jnp.mean(o_h * o_h, axis=-1, keepdims=True) + EPS) * gn_ref[...]
        a_h = (o_n * gate_g[:, h * GLA_DV:(h + 1) * GLA_DV]).astype(BF16)
        y = y + jnp.dot(a_h, wout_ref[S5_WIDTH + h * GLA_DV:S5_WIDTH + (h + 1) * GLA_DV, :],
                        preferred_element_type=F32)
    gate = mod_ref[:, 2 * D_MODEL:3 * D_MODEL]
    xn = x_ref[...] + gate * y
    if final:
        ms = jnp.mean(xn * xn, axis=-1, keepdims=True)
        xn = xn * lax.rsqrt(ms + EPS) * fin_ref[...]
    o_ref[...] = xn


def _finish(ys, zs, o_f, o_b, zg, x2, mods, layer, ctx_mod_row, w_glu, b_glu, gnorm, w_out, fin_gain, final,
            colmajor):
    rows = x2.shape[0]
    per_seq = rows // ys.shape[0]
    tb = min(ROW_BLOCK, per_seq)
    bps = per_seq // tb
    mod_row = (lambda i: layer * 8 + i // bps) if ctx_mod_row is None else (lambda i: layer * 8 + ctx_mod_row)
    rb = lambda w: pl.BlockSpec((tb, w), lambda i: (i, 0))
    full = lambda a: pl.BlockSpec(a.shape, lambda i: (0,) * a.ndim)
    of_layer = lambda a: pl.BlockSpec((None,) + a.shape[1:], lambda i: (layer,) + (0,) * (a.ndim - 1))
    if colmajor:
        assert tb % GRID_W == 0
        o_spec = pl.BlockSpec((None, GRID_W, tb // GRID_W, GLA_WIDTH), lambda i: (i // bps, 0, i % bps, 0))
    else:
        o_spec = rb(GLA_WIDTH)
    return pl.pallas_call(
        functools.partial(_finish_kernel, final=final, colmajor=colmajor),
        grid=(rows // tb,),
        in_specs=[pl.BlockSpec((None, S5_GROUPS, tb // S5_CHUNK, S5_ROW), lambda i: (i // bps, 0, i % bps, 0)),
                  rb(S5_WIDTH), o_spec, o_spec, rb(GLA_WIDTH), rb(D_MODEL),
                  pl.BlockSpec((None, 1, 3 * D_MODEL), lambda i: (mod_row(i), 0, 0)),
                  of_layer(w_glu), of_layer(b_glu), of_layer(gnorm), of_layer(w_out), full(fin_gain)],
        out_specs=rb(D_MODEL),
        out_shape=jax.ShapeDtypeStruct((rows, D_MODEL), F32),
        scratch_shapes=[pltpu.VMEM((S5_WIDTH // LANES, tb // S5_CHUNK * CHUNK_PITCH, LANES), F32),
                        pltpu.VMEM((GLA_HEADS, max(tb // GRID_W, 1) * GRID_ROW_PITCH, GLA_DV), F32)],
        compiler_params=_cparams("arbitrary"),
        name="finish",
    )(ys, zs, o_f, o_b, zg, x2, mods, w_glu, b_glu, gnorm, w_out, fin_gain)


def _layer(x_lat, x_ctx, mods, sw, layer, need_ctx_out, final):
    bsz, l, _ = x_lat.shape
    lc = x_ctx.shape[1]
    lat_row = None
    ctx_row = bsz

    pj_l = _inproj(x_lat.reshape(bsz * l, D_MODEL), mods, sw["gain"], sw["w_in"], layer, lat_row, bsz, colmajor=True)
    pj_c = _inproj(x_ctx.reshape(bsz * lc, D_MODEL), mods, sw["gain"], sw["w_in"], layer, ctx_row, bsz,
                   colmajor=False)
    xg_l, zs_l, q_l, k_l, v_l, zg_l, lr_l = pj_l
    xg_c, zs_c, q_c, k_c, v_c, zg_c, lr_c = pj_c

    ys_c, ys_l = _s5_branch(xg_c, xg_l, sw["s5"], layer, need_ctx_out)

    cols = lambda t: t.reshape(bsz, lc // GLA_COL, GLA_COL, t.shape[-1])
    s_zero = jnp.zeros((bsz, 2, GLA_KEY, GLA_DV), F32)
    ocf, ocb, s_ctx = _gla(cols(q_c), cols(k_c), cols(v_c), cols(lr_c), sw["wg"], sw["bg"], layer, s_zero)
    olf, olb, _ = _gla(q_l, k_l, v_l, lr_l, sw["wg"], sw["bg"], layer, s_ctx)

    fin_args = (sw["w_glu"], sw["b_glu"], sw["gnorm"], sw["w_out"], sw["fin"])
    x_lat_new = _finish(ys_l, zs_l, olf, olb, zg_l, x_lat.reshape(bsz * l, D_MODEL), mods, layer, lat_row,
                        *fin_args, final=final, colmajor=True).reshape(bsz, l, D_MODEL)
    x_ctx_new = x_ctx
    if need_ctx_out:
        x_ctx_new = _finish(ys_c, zs_c, ocf.reshape(bsz * lc, GLA_WIDTH), ocb.reshape(bsz * lc, GLA_WIDTH), zg_c,
                            x_ctx.reshape(bsz * lc, D_MODEL), mods, layer, ctx_row, *fin_args, final=False,
                            colmajor=False).reshape(bsz, lc, D_MODEL)
    return x_lat_new, x_ctx_new


def _stack_weights(norm_g, w_in, s5_lam_re, s5_lam_im, s5_log_dt, s5_b_re, s5_b_im, s5_c_re, s5_c_im, s5_d,
                   s5_w_glu, s5_b_glu, gla_w_gate, gla_b_gate, gla_norm_g, w_out, final_norm):
    w_pad = jnp.pad(w_in, ((0, 0), (0, 0), (0, IN_PAD - w_in.shape[-1]))).astype(BF16)
    wg = jnp.zeros((DEPTH, 2, LR_PAD, GLA_KEY), F32)
    for d in range(2):
        wg = wg.at[:, d, d * GLA_RANK:(d + 1) * GLA_RANK].set(gla_w_gate[:, d])
    lag_rows, out_rows, inj_rows, a1, a2 = jax.vmap(_s5_matrices)(
        s5_lam_re, s5_lam_im, s5_log_dt, s5_b_re, s5_b_im, s5_c_re, s5_c_im, s5_d)
    merged = lambda t: t.reshape((DEPTH * S5_GROUPS,) + t.shape[2:])
    out_rows, inj_rows = merged(out_rows), merged(inj_rows)
    toep_rows = _s5_lags(merged(lag_rows), out_rows)
    return {
        "gain": norm_g.reshape(DEPTH, 1, D_MODEL),
        "w_in": w_pad,
        "s5": (toep_rows, out_rows, inj_rows, a1, a2),
        "wg": wg.astype(BF16),
        "bg": gla_b_gate.reshape(DEPTH, 2, 1, GLA_KEY),
        "w_glu": s5_w_glu.astype(BF16),
        "b_glu": s5_b_glu.reshape(DEPTH, 1, S5_WIDTH),
        "gnorm": gla_norm_g.reshape(DEPTH, 1, GLA_DV),
        "w_out": w_out.astype(BF16),
        "fin": final_norm.reshape(1, D_MODEL),
    }


def kernel(x, c, ctx, c_ctx, norm_g, w_mod, b_mod, w_in, s5_lam_re, s5_lam_im, s5_log_dt, s5_b_re, s5_b_im,
           s5_c_re, s5_c_im, s5_d, s5_w_glu, s5_b_glu, gla_w_gate, gla_b_gate, gla_norm_g, w_out, final_norm):
    bsz = x.shape[0]
    cond8 = jnp.concatenate([c, c_ctx[None], jnp.zeros((8 - bsz - 1, D_MODEL), F32)], axis=0)
    mods = _modulation(cond8, w_mod, b_mod).reshape(DEPTH * 8, 1, 3 * D_MODEL)
    sw = _stack_weights(norm_g, w_in, s5_lam_re, s5_lam_im, s5_log_dt, s5_b_re, s5_b_im, s5_c_re, s5_c_im,
                        s5_d, s5_w_glu, s5_b_glu, gla_w_gate, gla_b_gate, gla_norm_g, w_out, final_norm)
    x_lat, x_ctx = x, ctx
    for i in range(DEPTH):
        last = i == DEPTH - 1
        x_lat, x_ctx = _layer(x_lat, x_ctx, mods, sw, i, need_ctx_out=not last, final=last)
    return x_lat
```

```python
import functools

import jax
import jax.numpy as jnp
from jax import lax
from jax.experimental import pallas as pl
from jax.experimental.pallas import tpu as pltpu

F32 = jnp.float32
BF16 = jnp.bfloat16

D_MODEL = 1024
DEPTH = 4
GRID_W = 64
EPS = 1e-6
LANES = 128
SUBLANES = 8
S5_WIDTH = 512
S5_GROUP = 16
S5_GROUPS = 32
S5_STATE = 64
S5_CHUNK = 16
S5_ROW = S5_CHUNK * S5_GROUP
S5_HALF = 2 * S5_STATE
S5_SLANES = 2 * S5_HALF
S5_SCAN_BLOCK = 32
S5_GROUP_BLOCK = 8
SLOTS = LANES // S5_GROUP
GLA_HEADS = 4
GLA_DV = 128
GLA_DK = 64
GLA_WIDTH = 512
GLA_KEY = 256
GLA_RANK = 16
GLA_NORMALIZER = 16.0
GLA_CHUNK = 64
GLA_COL = 128
LR_PAD = 128
IN_PAD = 2 * S5_WIDTH + 2 * GLA_KEY + 2 * GLA_WIDTH + LR_PAD

ROW_BLOCK = 512
GRID_ROWS_PER_BLOCK = ROW_BLOCK // GRID_W
CHUNK_PITCH = S5_CHUNK + SUBLANES
GRID_ROW_PITCH = GRID_W + SUBLANES
VMEM_LIMIT = 48 * 1024 * 1024


def _cparams(*sem):
    return pltpu.CompilerParams(dimension_semantics=sem, vmem_limit_bytes=VMEM_LIMIT)


def _mod_kernel(cond_ref, w_ref, b_ref, o_ref):
    cnd = cond_ref[...]
    a = cnd * jax.nn.sigmoid(cnd)
    o_ref[...] = jnp.dot(a, w_ref[...], preferred_element_type=F32,
                         precision=lax.Precision.HIGHEST) + b_ref[...]


def _modulation(cond8, w_mod, b_mod):
    nblk = 3
    return pl.pallas_call(
        _mod_kernel,
        grid=(DEPTH, nblk),
        in_specs=[
            pl.BlockSpec((8, D_MODEL), lambda l, j: (0, 0)),
            pl.BlockSpec((None, D_MODEL, D_MODEL), lambda l, j: (l, 0, j)),
            pl.BlockSpec((None, 1, D_MODEL), lambda l, j: (l, 0, j)),
        ],
        out_specs=pl.BlockSpec((None, 8, D_MODEL), lambda l, j: (l, 0, j)),
        out_shape=jax.ShapeDtypeStruct((DEPTH, 8, 3 * D_MODEL), F32),
        compiler_params=_cparams("arbitrary", "arbitrary"),
        name="modulation",
    )(cond8, w_mod, b_mod.reshape(DEPTH, 1, 3 * D_MODEL))


_IN_SEGS = (("zs", S5_WIDTH, False), ("q", GLA_KEY, True), ("k", GLA_KEY, True), ("v", GLA_WIDTH, True),
            ("zg", GLA_WIDTH, False), ("lr", LR_PAD, True))
_BF16_SEGS = ("zs", "zg")


def _swap_slots_with_index(vs):
    lane = lax.broadcasted_iota(jnp.int32, vs[0].shape, 1)
    for d in (4, 2, 1):
        sh = d * S5_GROUP
        low = (lane & sh) == 0
        nxt = list(vs)
        for i in range(SLOTS):
            if i & d == 0:
                a, b = vs[i], vs[i + d]
                nxt[i] = jnp.where(low, a, pltpu.roll(b, sh, 1))
                nxt[i + d] = jnp.where(low, pltpu.roll(a, LANES - sh, 1), b)
        vs = nxt
    return vs


def _inproj_kernel(x_ref, mod_ref, gain_ref, w_ref, xg_ref, *rest, colmajor):
    out_refs, u_scr, c_scr = rest[:-2], rest[-2], rest[-1]
    x = x_ref[...]
    ms = jnp.mean(x * x, axis=-1, keepdims=True)
    y = x * lax.rsqrt(ms + EPS) * gain_ref[...]
    shift = mod_ref[:, 0:D_MODEL]
    scale = mod_ref[:, D_MODEL:2 * D_MODEL]
    h = (y * (1.0 + scale) + shift).astype(BF16)
    u = jnp.dot(h, w_ref[:, 0:S5_WIDTH], preferred_element_type=F32)
    nchunk = u.shape[0] // S5_CHUNK
    for kt in range(S5_WIDTH // LANES):
        for c in range(nchunk):
            u_scr[kt, c * CHUNK_PITCH:c * CHUNK_PITCH + S5_CHUNK, :] = u[c * S5_CHUNK:(c + 1) * S5_CHUNK,
                                                                         kt * LANES:(kt + 1) * LANES]
    off = S5_WIDTH
    for (_, width, cm), o_ref in zip(_IN_SEGS, out_refs):
        val = jnp.dot(h, w_ref[:, off:off + width], preferred_element_type=F32)
        off += width
        if colmajor and cm:
            for kt in range(width // LANES):
                for r in range(GRID_ROWS_PER_BLOCK):
                    c_scr[kt, r * GRID_ROW_PITCH:r * GRID_ROW_PITCH + GRID_W, :] = val[r * GRID_W:(r + 1) * GRID_W,
                                                                                       kt * LANES:(kt + 1) * LANES]
            for w in range(GRID_W):
                for kt in range(width // LANES):
                    o_ref[w, :, kt * LANES:(kt + 1) * LANES] = c_scr[kt, pl.ds(w, GRID_ROWS_PER_BLOCK,
                                                                              stride=GRID_ROW_PITCH), :]
        else:
            o_ref[...] = val.astype(o_ref.dtype)
    for half in range(S5_CHUNK // SLOTS):
        for kt in range(S5_WIDTH // LANES):
            vs = [u_scr[kt, pl.ds(half * SLOTS + jj, nchunk, stride=CHUNK_PITCH), :] for jj in range(SLOTS)]
            ws = _swap_slots_with_index(vs)
            for qq in range(SLOTS):
                xg_ref[kt * SLOTS + qq, :, half * LANES:(half + 1) * LANES] = ws[qq].astype(BF16)


def _inproj(x2, mods, gain, w_pad, layer, ctx_mod_row, bsz, colmajor):
    rows = x2.shape[0]
    per_seq = rows // bsz
    tb = min(ROW_BLOCK, per_seq)
    bps = per_seq // tb
    mod_row = (lambda i: layer * 8 + i // bps) if ctx_mod_row is None else (lambda i: layer * 8 + ctx_mod_row)
    out_specs = [pl.BlockSpec((None, S5_GROUPS, tb // S5_CHUNK, S5_ROW), lambda i: (i // bps, 0, i % bps, 0))]
    out_shape = [jax.ShapeDtypeStruct((bsz, S5_GROUPS, per_seq // S5_CHUNK, S5_ROW), BF16)]
    for name, w, cm in _IN_SEGS:
        if colmajor and cm:
            assert tb == ROW_BLOCK
            out_specs.append(pl.BlockSpec((None, GRID_W, GRID_ROWS_PER_BLOCK, w),
                                          lambda i: (i // bps, 0, i % bps, 0)))
            out_shape.append(jax.ShapeDtypeStruct((bsz, GRID_W, per_seq // GRID_W, w), F32))
        else:
            out_specs.append(pl.BlockSpec((tb, w), lambda i: (i, 0)))
            out_shape.append(jax.ShapeDtypeStruct((rows, w), BF16 if name in _BF16_SEGS else F32))
    return pl.pallas_call(
        functools.partial(_inproj_kernel, colmajor=colmajor),
        grid=(rows // tb,),
        in_specs=[
            pl.BlockSpec((tb, D_MODEL), lambda i: (i, 0)),
            pl.BlockSpec((None, 1, 3 * D_MODEL), lambda i: (mod_row(i), 0, 0)),
            pl.BlockSpec((None, 1, D_MODEL), lambda i: (layer, 0, 0)),
            pl.BlockSpec((None, D_MODEL, IN_PAD), lambda i: (layer, 0, 0)),
        ],
        out_specs=out_specs,
        out_shape=out_shape,
        scratch_shapes=[pltpu.VMEM((S5_WIDTH // LANES, tb // S5_CHUNK * CHUNK_PITCH, LANES), F32),
                        pltpu.VMEM((GLA_WIDTH // LANES, GRID_ROWS_PER_BLOCK * GRID_ROW_PITCH, LANES), F32)],
        compiler_params=_cparams("arbitrary"),
        name="inproj",
    )(x2, mods, gain, w_pad)


def _s5_matrices(lam_re, lam_im, log_dt, b_re, b_im, c_re, c_im, d_skip):
    T = S5_CHUNK
    lam = lax.complex(lam_re.astype(F32), lam_im.astype(F32))
    dt = jnp.exp(log_dt.astype(F32))[..., None]
    ldt = lam * dt
    lam_bar = jnp.exp(ldt)
    b_c = lax.complex(b_re.astype(F32), b_im.astype(F32))
    b_bar = ((lam_bar - 1.0) / lam)[..., None] * b_c[None]
    c_c = lax.complex(c_re.astype(F32), c_im.astype(F32))
    steps = jnp.arange(T + 1, dtype=F32)
    pw = jnp.exp(ldt[:, None] * steps[None, :, None, None].astype(jnp.complex64))

    lanes = lambda parts: jnp.concatenate(parts, axis=-1)
    by_group = lambda p: p.transpose(1, 0, 2)

    qf = by_group(pw[0][:T][::-1])
    qb = by_group(pw[1][:T])
    bt = b_bar.transpose(0, 1, 3, 2)
    inj_rows = jnp.stack([lanes([qf.real, qf.real, qb.real, qb.real]),
                          lanes([-qf.imag, qf.imag, -qb.imag, qb.imag]),
                          lanes([bt[0].real, bt[0].imag, bt[1].real, bt[1].imag]),
                          lanes([bt[0].imag, bt[0].real, bt[1].imag, bt[1].real])], axis=1)

    pf = by_group(pw[0][1:T + 1])
    pb = by_group(pw[1][::-1][:T])
    out_rows = jnp.stack([lanes([pf.real, -pf.imag, pb.real, -pb.imag]),
                          lanes([-pf.imag, -pf.real, -pb.imag, -pb.real]),
                          lanes([c_c.real] * 4),
                          lanes([c_c.imag] * 4)], axis=1)

    kf_pw = by_group(pw[0][:T])
    kb_pw = by_group(pw[1][:T][::-1])
    skip = jnp.eye(S5_GROUP, dtype=F32)[None] * d_skip.astype(F32).reshape(S5_GROUPS, 1, S5_GROUP)
    lag_rows = jnp.stack([lanes([kf_pw.real, -kf_pw.imag, kb_pw.real, -kb_pw.imag]),
                          lanes([-kf_pw.imag, -kf_pw.real, -kb_pw.imag, -kb_pw.real]),
                          inj_rows[:, 2],
                          jnp.pad(skip, ((0, 0), (0, 0), (0, S5_ROW - S5_GROUP)))], axis=1)

    a_chunk = pw[:, T]
    a1 = jnp.concatenate([a_chunk.real, a_chunk.real], axis=-1)
    a2 = jnp.concatenate([-a_chunk.imag, a_chunk.imag], axis=-1)
    return lag_rows, out_rows, inj_rows, a1, a2


def _factor_rows_to_matrix(r_ref, gi, dst):
    s_a = r_ref[gi, 2]
    s_b = r_ref[gi, 3]
    for a in range(S5_CHUNK):
        rows = r_ref[gi, 0, a:a + 1, :] * s_a + r_ref[gi, 1, a:a + 1, :] * s_b
        dst[gi, a * S5_GROUP:(a + 1) * S5_GROUP, :] = rows.astype(BF16)


def _s5_inject_kernel(x_ref, r_ref, sf_ref, sb_ref, m_scr):
    for gi in range(S5_GROUP_BLOCK):
        _factor_rows_to_matrix(r_ref, gi, m_scr)
        s = jnp.dot(x_ref[gi], m_scr[gi], preferred_element_type=F32)
        sf_ref[gi] = s[:, 0:S5_HALF].astype(BF16)
        sb_ref[gi] = s[:, S5_HALF:S5_SLANES].astype(BF16)


def _s5_inject(xg, inj_rows, layer):
    bsz, _, nchunk, _ = xg.shape
    gb = S5_GROUP_BLOCK
    g0 = layer * (S5_GROUPS // gb)
    s_spec = pl.BlockSpec((None, gb, nchunk, S5_HALF), lambda b, g: (b, g, 0, 0))
    s_shape = jax.ShapeDtypeStruct((bsz, S5_GROUPS, nchunk, S5_HALF), BF16)
    return pl.pallas_call(
        _s5_inject_kernel,
        grid=(bsz, S5_GROUPS // gb),
        in_specs=[
            pl.BlockSpec((None, gb, nchunk, S5_ROW), lambda b, g: (b, g, 0, 0)),
            pl.BlockSpec((gb, 4, S5_GROUP, S5_SLANES), lambda b, g: (g0 + g, 0, 0, 0)),
        ],
        out_specs=[s_spec, s_spec],
        out_shape=[s_shape, s_shape],
        scratch_shapes=[pltpu.VMEM((gb, S5_ROW, S5_SLANES), BF16)],
        compiler_params=_cparams("arbitrary", "arbitrary"),
        name="s5_inject",
    )(xg, inj_rows)


def _s5_scan_kernel(sf_ref, sb_ref, a1_ref, a2_ref, h0_ref, hf_ref, hb_ref, hfin_ref, hf_scr, hb_scr, *, cpb):
    @pl.when(pl.program_id(0) == 0)
    def _():
        hf_scr[...] = h0_ref[0]
        hb_scr[...] = h0_ref[1]

    swap = lambda t: pltpu.roll(t, S5_STATE, 1)
    a1f = a1_ref[0]
    a2f = a2_ref[0]
    a1b = a1_ref[1]
    a2b = a2_ref[1]
    hf = hf_scr[...]
    hb = hb_scr[...]
    hf_sw = swap(hf)
    hb_sw = swap(hb)
    for c in range(cpb):
        hf_ref[:, c * S5_HALF:(c + 1) * S5_HALF] = hf.astype(BF16)
        s_f = sf_ref[:, c * S5_HALF:(c + 1) * S5_HALF].astype(F32)
        hf, hf_sw = a1f * hf + a2f * hf_sw + s_f, a1f * hf_sw - a2f * hf + swap(s_f)
        cb = cpb - 1 - c
        hb_ref[:, cb * S5_HALF:(cb + 1) * S5_HALF] = hb.astype(BF16)
        s_b = sb_ref[:, cb * S5_HALF:(cb + 1) * S5_HALF].astype(F32)
        hb, hb_sw = a1b * hb + a2b * hb_sw + s_b, a1b * hb_sw - a2b * hb + swap(s_b)
    hf_scr[...] = hf
    hb_scr[...] = hb

    @pl.when(pl.program_id(0) == pl.num_programs(0) - 1)
    def _():
        hfin_ref[0] = hf
        hfin_ref[1] = hb


def _s5_scan(s_f, s_b, a1, a2, h0):
    bsz, _, nchunk, _ = s_f.shape
    rows = bsz * S5_GROUPS
    cpb = min(S5_SCAN_BLOCK, nchunk)
    nblk = nchunk // cpb
    fwd = pl.BlockSpec((rows, cpb * S5_HALF), lambda i: (0, i))
    bwd = pl.BlockSpec((rows, cpb * S5_HALF), lambda i: (0, nblk - 1 - i))
    small = pl.BlockSpec((2, rows, S5_HALF), lambda i: (0, 0, 0))
    hf, hb, hfin = pl.pallas_call(
        functools.partial(_s5_scan_kernel, cpb=cpb),
        grid=(nblk,),
        in_specs=[fwd, bwd, small, small, small],
        out_specs=[fwd, bwd, small],
        out_shape=[jax.ShapeDtypeStruct((rows, nchunk * S5_HALF), BF16)] * 2
        + [jax.ShapeDtypeStruct((2, rows, S5_HALF), F32)],
        scratch_shapes=[pltpu.VMEM((rows, S5_HALF), F32), pltpu.VMEM((rows, S5_HALF), F32)],
        compiler_params=_cparams("arbitrary"),
        name="s5_scan",
    )(s_f.reshape(rows, nchunk * S5_HALF), s_b.reshape(rows, nchunk * S5_HALF), a1, a2, h0)
    shape4 = (bsz, S5_GROUPS, nchunk, S5_HALF)
    return hf.reshape(shape4), hb.reshape(shape4), hfin


def _dot_nt(a, b):
    return lax.dot_general(a, b, (((1,), (1,)), ((), ())), preferred_element_type=F32)


def _s5_lag_kernel(t_ref, r_ref, o_ref):
    hp = lax.Precision.HIGHEST
    nt = (((1,), (1,)), ((), ()))
    zero_rows = jnp.zeros((S5_GROUP, S5_ROW), F32)
    for gi in range(S5_GROUP_BLOCK):
        c_re = r_ref[gi, 2]
        c_im = r_ref[gi, 3]
        c_pw = jnp.concatenate([t_ref[gi, 0, t:t + 1, :] * c_re + t_ref[gi, 1, t:t + 1, :] * c_im
                                for t in range(S5_CHUNK)], axis=0)
        b_bar = t_ref[gi, 2]
        lag_f = lax.dot_general(b_bar[:, 0:S5_HALF], c_pw[:, 0:S5_HALF], nt, precision=hp,
                                preferred_element_type=F32) + t_ref[gi, 3]
        lag_b = lax.dot_general(b_bar[:, S5_HALF:S5_SLANES], c_pw[:, S5_HALF:S5_SLANES], nt, precision=hp,
                                preferred_element_type=F32)
        o_ref[gi, 0] = jnp.concatenate([zero_rows, lag_f], axis=1)
        o_ref[gi, 1] = jnp.concatenate([lag_b, zero_rows], axis=1)


def _s5_lags(lag_rows, out_rows):
    n = lag_rows.shape[0]
    gb = S5_GROUP_BLOCK
    spec = pl.BlockSpec((gb, 4, S5_GROUP, S5_SLANES), lambda g: (g, 0, 0, 0))
    return pl.pallas_call(
        _s5_lag_kernel,
        grid=(n // gb,),
        in_specs=[spec, spec],
        out_specs=pl.BlockSpec((gb, 2, S5_GROUP, 2 * S5_ROW), lambda g: (g, 0, 0, 0)),
        out_shape=jax.ShapeDtypeStruct((n, 2, S5_GROUP, 2 * S5_ROW), F32),
        compiler_params=_cparams("arbitrary"),
        name="s5_lags",
    )(lag_rows, out_rows)


def _s5_output_kernel(x_ref, hf_ref, hb_ref, t_ref, r_ref, y_ref, m_scr, c_scr):
    for gi in range(S5_GROUP_BLOCK):
        k_f = t_ref[gi, 0]
        k_b = t_ref[gi, 1]
        for i in range(S5_CHUNK):
            lo_f = S5_ROW - i * S5_GROUP
            lo_b = (S5_CHUNK - 1 - i) * S5_GROUP
            m_scr[gi, i * S5_GROUP:(i + 1) * S5_GROUP, :] = (k_f[:, lo_f:lo_f + S5_ROW]
                                                             + k_b[:, lo_b:lo_b + S5_ROW]).astype(BF16)
        _factor_rows_to_matrix(r_ref, gi, c_scr)
        y = jnp.dot(x_ref[gi], m_scr[gi], preferred_element_type=F32)
        y = y + _dot_nt(hf_ref[gi], c_scr[gi, :, 0:S5_HALF])
        y = y + _dot_nt(hb_ref[gi], c_scr[gi, :, S5_HALF:S5_SLANES])
        y_ref[gi] = y.astype(BF16)


def _s5_output(xg, hf, hb, toep_rows, out_rows, layer):
    bsz, _, nchunk, _ = xg.shape
    gb = S5_GROUP_BLOCK
    g0 = layer * (S5_GROUPS // gb)
    return pl.pallas_call(
        _s5_output_kernel,
        grid=(bsz, S5_GROUPS // gb),
        in_specs=[
            pl.BlockSpec((None, gb, nchunk, S5_ROW), lambda b, g: (b, g, 0, 0)),
            pl.BlockSpec((None, gb, nchunk, S5_HALF), lambda b, g: (b, g, 0, 0)),
            pl.BlockSpec((None, gb, nchunk, S5_HALF), lambda b, g: (b, g, 0, 0)),
            pl.BlockSpec((gb, 2, S5_GROUP, 2 * S5_ROW), lambda b, g: (g0 + g, 0, 0, 0)),
            pl.BlockSpec((gb, 4, S5_GROUP, S5_SLANES), lambda b, g: (g0 + g, 0, 0, 0)),
        ],
        out_specs=pl.BlockSpec((None, gb, nchunk, S5_ROW), lambda b, g: (b, g, 0, 0)),
        out_shape=jax.ShapeDtypeStruct((bsz, S5_GROUPS, nchunk, S5_ROW), BF16),
        scratch_shapes=[pltpu.VMEM((gb, S5_ROW, S5_ROW), BF16), pltpu.VMEM((gb, S5_ROW, S5_SLANES), BF16)],
        compiler_params=_cparams("arbitrary", "arbitrary"),
        name="s5_output",
    )(xg, hf, hb, toep_rows, out_rows)


def _s5_branch(xg_c, xg_l, mats, layer, need_ctx):
    toep_rows, out_rows, inj_rows, a1, a2 = mats
    bsz = xg_l.shape[0]
    a1r = jnp.tile(a1[layer], (1, bsz, 1))
    a2r = jnp.tile(a2[layer], (1, bsz, 1))
    sf_c, sb_c = _s5_inject(xg_c, inj_rows, layer)
    sf_l, sb_l = _s5_inject(xg_l, inj_rows, layer)
    hf_c, hb_c, h_ctx = _s5_scan(sf_c, sb_c, a1r, a2r, jnp.zeros_like(a1r))
    hf_l, hb_l, _ = _s5_scan(sf_l, sb_l, a1r, a2r, h_ctx)
    y_l = _s5_output(xg_l, hf_l, hb_l, toep_rows, out_rows, layer)
    y_c = _s5_output(xg_c, hf_c, hb_c, toep_rows, out_rows, layer) if need_ctx else None
    return y_c, y_l


def _gla_kernel(qf_ref, kf_ref, vf_ref, lrf_ref, qb_ref, kb_ref, vb_ref, lrb_ref, wg_ref, bg_ref, s0_ref,
                of_ref, ob_ref, sfin_ref, s_scr):
    w = pl.program_id(0)

    @pl.when(w == 0)
    def _():
        s_scr[...] = s0_ref[...]

    row = lax.broadcasted_iota(jnp.int32, (GLA_COL, GLA_COL), 0)
    col = lax.broadcasted_iota(jnp.int32, (GLA_COL, GLA_COL), 1)
    same_chunk = (row // GLA_CHUNK) == (col // GLA_CHUNK)
    r64 = lax.broadcasted_iota(jnp.int32, (GLA_CHUNK, GLA_CHUNK), 0)
    c64 = lax.broadcasted_iota(jnp.int32, (GLA_CHUNK, GLA_CHUNK), 1)
    dirs = (
        (qf_ref, kf_ref, vf_ref, lrf_ref, of_ref, jnp.where(same_chunk & (col <= row), 1.0, 0.0).astype(BF16),
         c64 <= r64),
        (qb_ref, kb_ref, vb_ref, lrb_ref, ob_ref, jnp.where(same_chunk & (col >= row), 1.0, 0.0).astype(BF16),
         c64 >= r64),
    )
    chains = [(b, d) for b in range(s_scr.shape[0]) for d in range(2)]

    pre = []
    for b, d in chains:
        q_ref, k_ref, _, lr_ref, _, tri, _ = dirs[d]
        z = jnp.dot(lr_ref[b].astype(BF16), wg_ref[d], preferred_element_type=F32) + bg_ref[d]
        g = (jnp.minimum(z, 0.0) - jnp.log1p(jnp.exp(-jnp.abs(z)))) * (1.0 / GLA_NORMALIZER)
        g_hi = g.astype(BF16)
        g_lo = (g - g_hi.astype(F32)).astype(BF16)
        gc = jnp.dot(tri, g_hi, preferred_element_type=F32) + jnp.dot(tri, g_lo, preferred_element_type=F32)
        qt = (q_ref[b] * jnp.exp(gc) * (GLA_DK ** -0.5)).astype(BF16)
        pre.append((qt, gc.T, k_ref[b].T))

    for pos in range(2):
        keys = []
        for (b, d), (qt, gct, kt) in zip(chains, pre):
            ci = pos if d == 0 else 1 - pos
            r0 = ci * GLA_CHUNK
            last = r0 + GLA_CHUNK - 1 if d == 0 else r0
            gct_c = gct[:, r0:r0 + GLA_CHUNK]
            dec = jnp.exp(gct[:, last:last + 1])
            k_in = kt[:, r0:r0 + GLA_CHUNK] * jnp.exp(-gct_c)
            keys.append((r0, dec, k_in.astype(BF16), (k_in * dec).astype(BF16)))
        for h in range(GLA_HEADS):
            ks = slice(h * GLA_DK, (h + 1) * GLA_DK)
            vs = slice(h * GLA_DV, (h + 1) * GLA_DV)
            scores = []
            for (b, d), (qt, _, _), (r0, _, k_in, _) in zip(chains, pre, keys):
                sc = jnp.dot(qt[r0:r0 + GLA_CHUNK, ks], k_in[ks, :], preferred_element_type=F32)
                scores.append(jnp.where(dirs[d][6], sc, 0.0).astype(BF16))
            for (b, d), (qt, _, _), (r0, dec, _, k_st), sc in zip(chains, pre, keys, scores):
                v_h = dirs[d][2][b, r0:r0 + GLA_CHUNK, vs].astype(BF16)
                s_h = s_scr[b, d, ks, :]
                o_h = jnp.dot(sc, v_h, preferred_element_type=F32)
                o_h = o_h + jnp.dot(qt[r0:r0 + GLA_CHUNK, ks], s_h.astype(BF16), preferred_element_type=F32)
                dirs[d][4][b, r0:r0 + GLA_CHUNK, vs] = o_h
                s_scr[b, d, ks, :] = dec[ks, :] * s_h + jnp.dot(k_st[ks, :], v_h, preferred_element_type=F32)

    @pl.when(w == pl.num_programs(0) - 1)
    def _():
        sfin_ref[...] = s_scr[...]


def _gla(q, k, v, lr, wg, bg, layer, s0):
    bsz, ncol = q.shape[:2]
    fwd = lambda w: (0, w, 0, 0)
    bwd = lambda w: (0, ncol - 1 - w, 0, 0)
    widths = (GLA_KEY, GLA_KEY, GLA_WIDTH, LR_PAD)
    state_spec = pl.BlockSpec((bsz, 2, GLA_KEY, GLA_DV), lambda w: (0, 0, 0, 0))
    in_specs = ([pl.BlockSpec((bsz, None, GLA_COL, wd), fwd) for wd in widths]
                + [pl.BlockSpec((bsz, None, GLA_COL, wd), bwd) for wd in widths]
                + [pl.BlockSpec((None, 2, LR_PAD, GLA_KEY), lambda w: (layer, 0, 0, 0)),
                   pl.BlockSpec((None, 2, 1, GLA_KEY), lambda w: (layer, 0, 0, 0)),
                   state_spec])
    o_shape = jax.ShapeDtypeStruct((bsz, ncol, GLA_COL, GLA_WIDTH), F32)
    return pl.pallas_call(
        _gla_kernel,
        grid=(ncol,),
        in_specs=in_specs,
        out_specs=[pl.BlockSpec((bsz, None, GLA_COL, GLA_WIDTH), fwd),
                   pl.BlockSpec((bsz, None, GLA_COL, GLA_WIDTH), bwd),
                   state_spec],
        out_shape=[o_shape, o_shape, jax.ShapeDtypeStruct((bsz, 2, GLA_KEY, GLA_DV), F32)],
        scratch_shapes=[pltpu.VMEM((bsz, 2, GLA_KEY, GLA_DV), F32)],
        compiler_params=_cparams("arbitrary"),
        name="gla",
    )(q, k, v, lr, q, k, v, lr, wg, bg, s0)


def _finish_kernel(ys_ref, zs_ref, of_ref, ob_ref, zg_ref, x_ref, mod_ref, wglu_ref, bglu_ref, gn_ref, wout_ref,
                   fin_ref, o_ref, ys_scr, o_scr, *, final, colmajor):
    nchunk = ys_ref.shape[1]
    for half in range(S5_CHUNK // SLOTS):
        for kt in range(S5_WIDTH // LANES):
            ws = [ys_ref[kt * SLOTS + qq, :, half * LANES:(half + 1) * LANES].astype(F32) for qq in range(SLOTS)]
            vs = _swap_slots_with_index(ws)
            for jj in range(SLOTS):
                ys_scr[kt, pl.ds(half * SLOTS + jj, nchunk, stride=CHUNK_PITCH), :] = vs[jj]
    ys = jnp.concatenate(
        [jnp.concatenate([ys_scr[kt, c * CHUNK_PITCH:c * CHUNK_PITCH + S5_CHUNK, :] for c in range(nchunk)], axis=0)
         for kt in range(S5_WIDTH // LANES)], axis=1)
    s = jax.nn.gelu(ys, approximate=True)
    t = jnp.dot(s.astype(BF16), wglu_ref[...], preferred_element_type=F32) + bglu_ref[...]
    zs = zs_ref[...].astype(F32)
    a_s = (s * jax.nn.sigmoid(t)) * (zs * jax.nn.sigmoid(zs))
    y = jnp.dot(a_s.astype(BF16), wout_ref[0:S5_WIDTH, :], preferred_element_type=F32)

    if colmajor:
        grid_rows = of_ref.shape[1]
        for w in range(GRID_W):
            o_w = of_ref[w] + ob_ref[w]
            for h in range(GLA_HEADS):
                o_scr[h, pl.ds(w, grid_rows, stride=GRID_ROW_PITCH), :] = o_w[:, h * GLA_DV:(h + 1) * GLA_DV]
        o_heads = [jnp.concatenate([o_scr[h, r * GRID_ROW_PITCH:r * GRID_ROW_PITCH + GRID_W, :]
                                    for r in range(grid_rows)], axis=0) for h in range(GLA_HEADS)]
    else:
        o_sum = of_ref[...] + ob_ref[...]
        o_heads = [o_sum[:, h * GLA_DV:(h + 1) * GLA_DV] for h in range(GLA_HEADS)]
    zg = zg_ref[...].astype(F32)
    gate_g = zg * jax.nn.sigmoid(zg)
    for h in range(GLA_HEADS):
        o_h = o_heads[h]
        o_n = o_h * lax.rsqrt(jnp.mean(o_h * o_h, axis=-1, keepdims=True) + EPS) * gn_ref[...]
        a_h = (o_n * gate_g[:, h * GLA_DV:(h + 1) * GLA_DV]).astype(BF16)
        y = y + jnp.dot(a_h, wout_ref[S5_WIDTH + h * GLA_DV:S5_WIDTH + (h + 1) * GLA_DV, :],
                        preferred_element_type=F32)
    gate = mod_ref[:, 2 * D_MODEL:3 * D_MODEL]
    xn = x_ref[...] + gate * y
    if final:
        ms = jnp.mean(xn * xn, axis=-1, keepdims=True)
        xn = xn * lax.rsqrt(ms + EPS) * fin_ref[...]
    o_ref[...] = xn


def _finish(ys, zs, o_f, o_b, zg, x2, mods, layer, ctx_mod_row, w_glu, b_glu, gnorm, w_out, fin_gain, final,
            colmajor):
    rows = x2.shape[0]
    per_seq = rows // ys.shape[0]
    tb = min(ROW_BLOCK, per_seq)
    bps = per_seq // tb
    mod_row = (lambda i: layer * 8 + i // bps) if ctx_mod_row is None else (lambda i: layer * 8 + ctx_mod_row)
    rb = lambda w: pl.BlockSpec((tb, w), lambda i: (i, 0))
    full = lambda a: pl.BlockSpec(a.shape, lambda i: (0,) * a.ndim)
    of_layer = lambda a: pl.BlockSpec((None,) + a.shape[1:], lambda i: (layer,) + (0,) * (a.ndim - 1))
    if colmajor:
        assert tb % GRID_W == 0
        o_spec = pl.BlockSpec((None, GRID_W, tb // GRID_W, GLA_WIDTH), lambda i: (i // bps, 0, i % bps, 0))
    else:
        o_spec = rb(GLA_WIDTH)
    return pl.pallas_call(
        functools.partial(_finish_kernel, final=final, colmajor=colmajor),
        grid=(rows // tb,),
        in_specs=[pl.BlockSpec((None, S5_GROUPS, tb // S5_CHUNK, S5_ROW), lambda i: (i // bps, 0, i % bps, 0)),
                  rb(S5_WIDTH), o_spec, o_spec, rb(GLA_WIDTH), rb(D_MODEL),
                  pl.BlockSpec((None, 1, 3 * D_MODEL), lambda i: (mod_row(i), 0, 0)),
                  of_layer(w_glu), of_layer(b_glu), of_layer(gnorm), of_layer(w_out), full(fin_gain)],
        out_specs=rb(D_MODEL),
        out_shape=jax.ShapeDtypeStruct((rows, D_MODEL), F32),
        scratch_shapes=[pltpu.VMEM((S5_WIDTH // LANES, tb // S5_CHUNK * CHUNK_PITCH, LANES), F32),
                        pltpu.VMEM((GLA_HEADS, max(tb // GRID_W, 1) * GRID_ROW_PITCH, GLA_DV), F32)],
        compiler_params=_cparams("arbitrary"),
        name="finish",
    )(ys, zs, o_f, o_b, zg, x2, mods, w_glu, b_glu, gnorm, w_out, fin_gain)


def _layer(x_lat, x_ctx, mods, sw, layer, need_ctx_out, final):
    bsz, l, _ = x_lat.shape
    lc = x_ctx.shape[1]
    lat_row = None
    ctx_row = bsz

    pj_l = _inproj(x_lat.reshape(bsz * l, D_MODEL), mods, sw["gain"], sw["w_in"], layer, lat_row, bsz, colmajor=True)
    pj_c = _inproj(x_ctx.reshape(bsz * lc, D_MODEL), mods, sw["gain"], sw["w_in"], layer, ctx_row, bsz,
                   colmajor=False)
    xg_l, zs_l, q_l, k_l, v_l, zg_l, lr_l = pj_l
    xg_c, zs_c, q_c, k_c, v_c, zg_c, lr_c = pj_c

    ys_c, ys_l = _s5_branch(xg_c, xg_l, sw["s5"], layer, need_ctx_out)

    cols = lambda t: t.reshape(bsz, lc // GLA_COL, GLA_COL, t.shape[-1])
    s_zero = jnp.zeros((bsz, 2, GLA_KEY, GLA_DV), F32)
    ocf, ocb, s_ctx = _gla(cols(q_c), cols(k_c), cols(v_c), cols(lr_c), sw["wg"], sw["bg"], layer, s_zero)
    olf, olb, _ = _gla(q_l, k_l, v_l, lr_l, sw["wg"], sw["bg"], layer, s_ctx)

    fin_args = (sw["w_glu"], sw["b_glu"], sw["gnorm"], sw["w_out"], sw["fin"])
    x_lat_new = _finish(ys_l, zs_l, olf, olb, zg_l, x_lat.reshape(bsz * l, D_MODEL), mods, layer, lat_row,
                        *fin_args, final=final, colmajor=True).reshape(bsz, l, D_MODEL)
    x_ctx_new = x_ctx
    if need_ctx_out:
        x_ctx_new = _finish(ys_c, zs_c, ocf.reshape(bsz * lc, GLA_WIDTH), ocb.reshape(bsz * lc, GLA_WIDTH), zg_c,
                            x_ctx.reshape(bsz * lc, D_MODEL), mods, layer, ctx_row, *fin_args, final=False,
                            colmajor=False).reshape(bsz, lc, D_MODEL)
    return x_lat_new, x_ctx_new


def _stack_weights(norm_g, w_in, s5_lam_re, s5_lam_im, s5_log_dt, s5_b_re, s5_b_im, s5_c_re, s5_c_im, s5_d,
                   s5_w_glu, s5_b_glu, gla_w_gate, gla_b_gate, gla_norm_g, w_out, final_norm):
    w_pad = jnp.pad(w_in, ((0, 0), (0, 0), (0, IN_PAD - w_in.shape[-1]))).astype(BF16)
    wg = jnp.zeros((DEPTH, 2, LR_PAD, GLA_KEY), F32)
    for d in range(2):
        wg = wg.at[:, d, d * GLA_RANK:(d + 1) * GLA_RANK].set(gla_w_gate[:, d])
    lag_rows, out_rows, inj_rows, a1, a2 = jax.vmap(_s5_matrices)(
        s5_lam_re, s5_lam_im, s5_log_dt, s5_b_re, s5_b_im, s5_c_re, s5_c_im, s5_d)
    merged = lambda t: t.reshape((DEPTH * S5_GROUPS,) + t.shape[2:])
    out_rows, inj_rows = merged(out_rows), merged(inj_rows)
    toep_rows = _s5_lags(merged(lag_rows), out_rows)
    return {
        "gain": norm_g.reshape(DEPTH, 1, D_MODEL),
        "w_in": w_pad,
        "s5": (toep_rows, out_rows, inj_rows, a1, a2),
        "wg": wg.astype(BF16),
        "bg": gla_b_gate.reshape(DEPTH, 2, 1, GLA_KEY),
        "w_glu": s5_w_glu.astype(BF16),
        "b_glu": s5_b_glu.reshape(DEPTH, 1, S5_WIDTH),
        "gnorm": gla_norm_g.reshape(DEPTH, 1, GLA_DV),
        "w_out": w_out.astype(BF16),
        "fin": final_norm.reshape(1, D_MODEL),
    }


def kernel(x, c, ctx, c_ctx, norm_g, w_mod, b_mod, w_in, s5_lam_re, s5_lam_im, s5_log_dt, s5_b_re, s5_b_im,
           s5_c_re, s5_c_im, s5_d, s5_w_glu, s5_b_glu, gla_w_gate, gla_b_gate, gla_norm_g, w_out, final_norm):
    bsz = x.shape[0]
    cond8 = jnp.concatenate([c, c_ctx[None], jnp.zeros((8 - bsz - 1, D_MODEL), F32)], axis=0)
    mods = _modulation(cond8, w_mod, b_mod).reshape(DEPTH * 8, 1, 3 * D_MODEL)
    sw = _stack_weights(norm_g, w_in, s5_lam_re, s5_lam_im, s5_log_dt, s5_b_re, s5_b_im, s5_c_re, s5_c_im,
                        s5_d, s5_w_glu, s5_b_glu, gla_w_gate, gla_b_gate, gla_norm_g, w_out, final_norm)
    x_lat, x_ctx = x, ctx
    for i in range(DEPTH):
        last = i == DEPTH - 1
        x_lat, x_ctx = _layer(x_lat, x_ctx, mods, sw, i, need_ctx_out=not last, final=last)
    return x_lat
```

```python
import functools

import jax
import jax.numpy as jnp
from jax import lax
from jax.experimental import pallas as pl
from jax.experimental.pallas import tpu as pltpu

F32 = jnp.float32
BF16 = jnp.bfloat16

D_MODEL = 1024
DEPTH = 4
GRID_W = 64
EPS = 1e-6
LANES = 128
SUBLANES = 8
S5_WIDTH = 512
S5_GROUP = 16
S5_GROUPS = 32
S5_STATE = 64
S5_CHUNK = 16
S5_ROW = S5_CHUNK * S5_GROUP
S5_HALF = 2 * S5_STATE
S5_SLANES = 2 * S5_HALF
S5_SCAN_BLOCK = 64
S5_GROUP_BLOCK = 8
SLOTS = LANES // S5_GROUP
GLA_HEADS = 4
GLA_DV = 128
GLA_DK = 64
GLA_WIDTH = 512
GLA_KEY = 256
GLA_RANK = 16
GLA_NORMALIZER = 16.0
GLA_CHUNK = 64
GLA_COL = 128
LR_PAD = 128
IN_PAD = 2 * S5_WIDTH + 2 * GLA_KEY + 2 * GLA_WIDTH + LR_PAD

ROW_BLOCK = 512
GRID_ROWS_PER_BLOCK = ROW_BLOCK // GRID_W
CHUNK_PITCH = S5_CHUNK + SUBLANES
GRID_ROW_PITCH = GRID_W + SUBLANES
VMEM_LIMIT = 48 * 1024 * 1024


def _cparams(*sem):
    return pltpu.CompilerParams(dimension_semantics=sem, vmem_limit_bytes=VMEM_LIMIT)


def _mod_kernel(cond_ref, w_ref, b_ref, o_ref):
    cnd = cond_ref[...]
    a = cnd * jax.nn.sigmoid(cnd)
    o_ref[...] = jnp.dot(a, w_ref[...], preferred_element_type=F32,
                         precision=lax.Precision.HIGHEST) + b_ref[...]


def _modulation(cond8, w_mod, b_mod):
    nblk = 3
    return pl.pallas_call(
        _mod_kernel,
        grid=(DEPTH, nblk),
        in_specs=[
            pl.BlockSpec((8, D_MODEL), lambda l, j: (0, 0)),
            pl.BlockSpec((None, D_MODEL, D_MODEL), lambda l, j: (l, 0, j)),
            pl.BlockSpec((None, 1, D_MODEL), lambda l, j: (l, 0, j)),
        ],
        out_specs=pl.BlockSpec((None, 8, D_MODEL), lambda l, j: (l, 0, j)),
        out_shape=jax.ShapeDtypeStruct((DEPTH, 8, 3 * D_MODEL), F32),
        compiler_params=_cparams("arbitrary", "arbitrary"),
        name="modulation",
    )(cond8, w_mod, b_mod.reshape(DEPTH, 1, 3 * D_MODEL))


_IN_SEGS = (("zs", S5_WIDTH, False), ("q", GLA_KEY, True), ("k", GLA_KEY, True), ("v", GLA_WIDTH, True),
            ("zg", GLA_WIDTH, False), ("lr", LR_PAD, True))
_BF16_SEGS = ("zs", "zg")


def _swap_slots_with_index(vs):
    lane = lax.broadcasted_iota(jnp.int32, vs[0].shape, 1)
    for d in (4, 2, 1):
        sh = d * S5_GROUP
        low = (lane & sh) == 0
        nxt = list(vs)
        for i in range(SLOTS):
            if i & d == 0:
                a, b = vs[i], vs[i + d]
                nxt[i] = jnp.where(low, a, pltpu.roll(b, sh, 1))
                nxt[i + d] = jnp.where(low, pltpu.roll(a, LANES - sh, 1), b)
        vs = nxt
    return vs


def _inproj_kernel(x_ref, mod_ref, gain_ref, w_ref, xg_ref, *rest, colmajor):
    out_refs, u_scr, c_scr = rest[:-2], rest[-2], rest[-1]
    x = x_ref[...]
    ms = jnp.mean(x * x, axis=-1, keepdims=True)
    y = x * lax.rsqrt(ms + EPS) * gain_ref[...]
    shift = mod_ref[:, 0:D_MODEL]
    scale = mod_ref[:, D_MODEL:2 * D_MODEL]
    h = (y * (1.0 + scale) + shift).astype(BF16)
    u = jnp.dot(h, w_ref[:, 0:S5_WIDTH], preferred_element_type=F32)
    nchunk = u.shape[0] // S5_CHUNK
    for kt in range(S5_WIDTH // LANES):
        for c in range(nchunk):
            u_scr[kt, c * CHUNK_PITCH:c * CHUNK_PITCH + S5_CHUNK, :] = u[c * S5_CHUNK:(c + 1) * S5_CHUNK,
                                                                         kt * LANES:(kt + 1) * LANES]
    off = S5_WIDTH
    for (_, width, cm), o_ref in zip(_IN_SEGS, out_refs):
        val = jnp.dot(h, w_ref[:, off:off + width], preferred_element_type=F32)
        off += width
        if colmajor and cm:
            for kt in range(width // LANES):
                for r in range(GRID_ROWS_PER_BLOCK):
                    c_scr[kt, r * GRID_ROW_PITCH:r * GRID_ROW_PITCH + GRID_W, :] = val[r * GRID_W:(r + 1) * GRID_W,
                                                                                       kt * LANES:(kt + 1) * LANES]
            for w in range(GRID_W):
                for kt in range(width // LANES):
                    o_ref[w, :, kt * LANES:(kt + 1) * LANES] = c_scr[kt, pl.ds(w, GRID_ROWS_PER_BLOCK,
                                                                              stride=GRID_ROW_PITCH), :]
        else:
            o_ref[...] = val.astype(o_ref.dtype)
    for half in range(S5_CHUNK // SLOTS):
        for kt in range(S5_WIDTH // LANES):
            vs = [u_scr[kt, pl.ds(half * SLOTS + jj, nchunk, stride=CHUNK_PITCH), :] for jj in range(SLOTS)]
            ws = _swap_slots_with_index(vs)
            for qq in range(SLOTS):
                xg_ref[kt * SLOTS + qq, :, half * LANES:(half + 1) * LANES] = ws[qq].astype(BF16)


def _inproj(x2, mods, gain, w_pad, layer, ctx_mod_row, bsz, colmajor):
    rows = x2.shape[0]
    per_seq = rows // bsz
    tb = min(ROW_BLOCK, per_seq)
    bps = per_seq // tb
    mod_row = (lambda i: layer * 8 + i // bps) if ctx_mod_row is None else (lambda i: layer * 8 + ctx_mod_row)
    out_specs = [pl.BlockSpec((None, S5_GROUPS, tb // S5_CHUNK, S5_ROW), lambda i: (i // bps, 0, i % bps, 0))]
    out_shape = [jax.ShapeDtypeStruct((bsz, S5_GROUPS, per_seq // S5_CHUNK, S5_ROW), BF16)]
    for name, w, cm in _IN_SEGS:
        if colmajor and cm:
            assert tb == ROW_BLOCK
            out_specs.append(pl.BlockSpec((None, GRID_W, GRID_ROWS_PER_BLOCK, w),
                                          lambda i: (i // bps, 0, i % bps, 0)))
            out_shape.append(jax.ShapeDtypeStruct((bsz, GRID_W, per_seq // GRID_W, w), F32))
        else:
            out_specs.append(pl.BlockSpec((tb, w), lambda i: (i, 0)))
            out_shape.append(jax.ShapeDtypeStruct((rows, w), BF16 if name in _BF16_SEGS else F32))
    return pl.pallas_call(
        functools.partial(_inproj_kernel, colmajor=colmajor),
        grid=(rows // tb,),
        in_specs=[
            pl.BlockSpec((tb, D_MODEL), lambda i: (i, 0)),
            pl.BlockSpec((None, 1, 3 * D_MODEL), lambda i: (mod_row(i), 0, 0)),
            pl.BlockSpec((None, 1, D_MODEL), lambda i: (layer, 0, 0)),
            pl.BlockSpec((None, D_MODEL, IN_PAD), lambda i: (layer, 0, 0)),
        ],
        out_specs=out_specs,
        out_shape=out_shape,
        scratch_shapes=[pltpu.VMEM((S5_WIDTH // LANES, tb // S5_CHUNK * CHUNK_PITCH, LANES), F32),
                        pltpu.VMEM((GLA_WIDTH // LANES, GRID_ROWS_PER_BLOCK * GRID_ROW_PITCH, LANES), F32)],
        compiler_params=_cparams("arbitrary"),
        name="inproj",
    )(x2, mods, gain, w_pad)


def _s5_matrices(lam_re, lam_im, log_dt, b_re, b_im, c_re, c_im, d_skip):
    T = S5_CHUNK
    lam = lax.complex(lam_re.astype(F32), lam_im.astype(F32))
    dt = jnp.exp(log_dt.astype(F32))[..., None]
    ldt = lam * dt
    lam_bar = jnp.exp(ldt)
    b_c = lax.complex(b_re.astype(F32), b_im.astype(F32))
    b_bar = ((lam_bar - 1.0) / lam)[..., None] * b_c[None]
    c_c = lax.complex(c_re.astype(F32), c_im.astype(F32))
    steps = jnp.arange(T + 1, dtype=F32)
    pw = jnp.exp(ldt[:, None] * steps[None, :, None, None].astype(jnp.complex64))

    lanes = lambda parts: jnp.concatenate(parts, axis=-1)
    by_group = lambda p: p.transpose(1, 0, 2)

    qf = by_group(pw[0][:T][::-1])
    qb = by_group(pw[1][:T])
    bt = b_bar.transpose(0, 1, 3, 2)
    inj_rows = jnp.stack([lanes([qf.real, qf.real, qb.real, qb.real]),
                          lanes([-qf.imag, qf.imag, -qb.imag, qb.imag]),
                          lanes([bt[0].real, bt[0].imag, bt[1].real, bt[1].imag]),
                          lanes([bt[0].imag, bt[0].real, bt[1].imag, bt[1].real])], axis=1)

    pf = by_group(pw[0][1:T + 1])
    pb = by_group(pw[1][::-1][:T])
    out_rows = jnp.stack([lanes([pf.real, -pf.imag, pb.real, -pb.imag]),
                          lanes([-pf.imag, -pf.real, -pb.imag, -pb.real]),
                          lanes([c_c.real] * 4),
                          lanes([c_c.imag] * 4)], axis=1)

    kf_pw = by_group(pw[0][:T])
    kb_pw = by_group(pw[1][:T][::-1])
    skip = jnp.eye(S5_GROUP, dtype=F32)[None] * d_skip.astype(F32).reshape(S5_GROUPS, 1, S5_GROUP)
    lag_rows = jnp.stack([lanes([kf_pw.real, -kf_pw.imag, kb_pw.real, -kb_pw.imag]),
                          lanes([-kf_pw.imag, -kf_pw.real, -kb_pw.imag, -kb_pw.real]),
                          inj_rows[:, 2],
                          jnp.pad(skip, ((0, 0), (0, 0), (0, S5_ROW - S5_GROUP)))], axis=1)

    a_chunk = pw[:, T]
    a1 = jnp.concatenate([a_chunk.real, a_chunk.real], axis=-1)
    a2 = jnp.concatenate([-a_chunk.imag, a_chunk.imag], axis=-1)
    return lag_rows, out_rows, inj_rows, a1, a2


def _factor_rows_to_matrix(r_ref, gi, dst):
    s_a = r_ref[gi, 2]
    s_b = r_ref[gi, 3]
    for a in range(S5_CHUNK):
        rows = r_ref[gi, 0, a:a + 1, :] * s_a + r_ref[gi, 1, a:a + 1, :] * s_b
        dst[gi, a * S5_GROUP:(a + 1) * S5_GROUP, :] = rows.astype(BF16)


def _s5_inject_kernel(x_ref, r_ref, sf_ref, sb_ref, m_scr):
    for gi in range(S5_GROUP_BLOCK):
        _factor_rows_to_matrix(r_ref, gi, m_scr)
        s = jnp.dot(x_ref[gi], m_scr[gi], preferred_element_type=F32)
        sf_ref[gi] = s[:, 0:S5_HALF].astype(BF16)
        sb_ref[gi] = s[:, S5_HALF:S5_SLANES].astype(BF16)


def _s5_inject(xg, inj_rows, layer):
    bsz, _, nchunk, _ = xg.shape
    gb = S5_GROUP_BLOCK
    g0 = layer * (S5_GROUPS // gb)
    s_spec = pl.BlockSpec((None, gb, nchunk, S5_HALF), lambda b, g: (b, g, 0, 0))
    s_shape = jax.ShapeDtypeStruct((bsz, S5_GROUPS, nchunk, S5_HALF), BF16)
    return pl.pallas_call(
        _s5_inject_kernel,
        grid=(bsz, S5_GROUPS // gb),
        in_specs=[
            pl.BlockSpec((None, gb, nchunk, S5_ROW), lambda b, g: (b, g, 0, 0)),
            pl.BlockSpec((gb, 4, S5_GROUP, S5_SLANES), lambda b, g: (g0 + g, 0, 0, 0)),
        ],
        out_specs=[s_spec, s_spec],
        out_shape=[s_shape, s_shape],
        scratch_shapes=[pltpu.VMEM((gb, S5_ROW, S5_SLANES), BF16)],
        compiler_params=_cparams("arbitrary", "arbitrary"),
        name="s5_inject",
    )(xg, inj_rows)


def _s5_scan_kernel(sf_ref, sb_ref, a1_ref, a2_ref, h0_ref, hf_ref, hb_ref, hfin_ref, hf_scr, hb_scr, *, cpb):
    @pl.when(pl.program_id(0) == 0)
    def _():
        hf_scr[...] = h0_ref[0]
        hb_scr[...] = h0_ref[1]

    swap = lambda t: pltpu.roll(t, S5_STATE, 1)
    a1f = a1_ref[0]
    a2f = a2_ref[0]
    a1b = a1_ref[1]
    a2b = a2_ref[1]
    hf = hf_scr[...]
    hb = hb_scr[...]
    hf_sw = swap(hf)
    hb_sw = swap(hb)
    for c in range(cpb):
        hf_ref[:, c * S5_HALF:(c + 1) * S5_HALF] = hf.astype(BF16)
        s_f = sf_ref[:, c * S5_HALF:(c + 1) * S5_HALF].astype(F32)
        hf, hf_sw = a1f * hf + a2f * hf_sw + s_f, a1f * hf_sw - a2f * hf + swap(s_f)
        cb = cpb - 1 - c
        hb_ref[:, cb * S5_HALF:(cb + 1) * S5_HALF] = hb.astype(BF16)
        s_b = sb_ref[:, cb * S5_HALF:(cb + 1) * S5_HALF].astype(F32)
        hb, hb_sw = a1b * hb + a2b * hb_sw + s_b, a1b * hb_sw - a2b * hb + swap(s_b)
    hf_scr[...] = hf
    hb_scr[...] = hb

    @pl.when(pl.program_id(0) == pl.num_programs(0) - 1)
    def _():
        hfin_ref[0] = hf
        hfin_ref[1] = hb


def _s5_scan(s_f, s_b, a1, a2, h0):
    bsz, _, nchunk, _ = s_f.shape
    rows = bsz * S5_GROUPS
    cpb = min(S5_SCAN_BLOCK, nchunk)
    nblk = nchunk // cpb
    fwd = pl.BlockSpec((rows, cpb * S5_HALF), lambda i: (0, i))
    bwd = pl.BlockSpec((rows, cpb * S5_HALF), lambda i: (0, nblk - 1 - i))
    small = pl.BlockSpec((2, rows, S5_HALF), lambda i: (0, 0, 0))
    hf, hb, hfin = pl.pallas_call(
        functools.partial(_s5_scan_kernel, cpb=cpb),
        grid=(nblk,),
        in_specs=[fwd, bwd, small, small, small],
        out_specs=[fwd, bwd, small],
        out_shape=[jax.ShapeDtypeStruct((rows, nchunk * S5_HALF), BF16)] * 2
        + [jax.ShapeDtypeStruct((2, rows, S5_HALF), F32)],
        scratch_shapes=[pltpu.VMEM((rows, S5_HALF), F32), pltpu.VMEM((rows, S5_HALF), F32)],
        compiler_params=_cparams("arbitrary"),
        name="s5_scan",
    )(s_f.reshape(rows, nchunk * S5_HALF), s_b.reshape(rows, nchunk * S5_HALF), a1, a2, h0)
    shape4 = (bsz, S5_GROUPS, nchunk, S5_HALF)
    return hf.reshape(shape4), hb.reshape(shape4), hfin


def _dot_nt(a, b):
    return lax.dot_general(a, b, (((1,), (1,)), ((), ())), preferred_element_type=F32)


def _s5_lag_kernel(t_ref, r_ref, o_ref):
    hp = lax.Precision.HIGHEST
    nt = (((1,), (1,)), ((), ()))
    zero_rows = jnp.zeros((S5_GROUP, S5_ROW), F32)
    for gi in range(S5_GROUP_BLOCK):
        c_re = r_ref[gi, 2]
        c_im = r_ref[gi, 3]
        c_pw = jnp.concatenate([t_ref[gi, 0, t:t + 1, :] * c_re + t_ref[gi, 1, t:t + 1, :] * c_im
                                for t in range(S5_CHUNK)], axis=0)
        b_bar = t_ref[gi, 2]
        lag_f = lax.dot_general(b_bar[:, 0:S5_HALF], c_pw[:, 0:S5_HALF], nt, precision=hp,
                                preferred_element_type=F32) + t_ref[gi, 3]
        lag_b = lax.dot_general(b_bar[:, S5_HALF:S5_SLANES], c_pw[:, S5_HALF:S5_SLANES], nt, precision=hp,
                                preferred_element_type=F32)
        o_ref[gi, 0] = jnp.concatenate([zero_rows, lag_f], axis=1)
        o_ref[gi, 1] = jnp.concatenate([lag_b, zero_rows], axis=1)


def _s5_lags(lag_rows, out_rows):
    n = lag_rows.shape[0]
    gb = S5_GROUP_BLOCK
    spec = pl.BlockSpec((gb, 4, S5_GROUP, S5_SLANES), lambda g: (g, 0, 0, 0))
    return pl.pallas_call(
        _s5_lag_kernel,
        grid=(n // gb,),
        in_specs=[spec, spec],
        out_specs=pl.BlockSpec((gb, 2, S5_GROUP, 2 * S5_ROW), lambda g: (g, 0, 0, 0)),
        out_shape=jax.ShapeDtypeStruct((n, 2, S5_GROUP, 2 * S5_ROW), F32),
        compiler_params=_cparams("arbitrary"),
        name="s5_lags",
    )(lag_rows, out_rows)


def _s5_output_kernel(x_ref, hf_ref, hb_ref, t_ref, r_ref, y_ref, m_scr, c_scr):
    for gi in range(S5_GROUP_BLOCK):
        k_f = t_ref[gi, 0]
        k_b = t_ref[gi, 1]
        for i in range(S5_CHUNK):
            lo_f = S5_ROW - i * S5_GROUP
            lo_b = (S5_CHUNK - 1 - i) * S5_GROUP
            m_scr[gi, i * S5_GROUP:(i + 1) * S5_GROUP, :] = (k_f[:, lo_f:lo_f + S5_ROW]
                                                             + k_b[:, lo_b:lo_b + S5_ROW]).astype(BF16)
        _factor_rows_to_matrix(r_ref, gi, c_scr)
        y = jnp.dot(x_ref[gi], m_scr[gi], preferred_element_type=F32)
        y = y + _dot_nt(jnp.concatenate([hf_ref[gi], hb_ref[gi]], axis=1), c_scr[gi])
        y_ref[gi] = y.astype(BF16)


def _s5_output(xg, hf, hb, toep_rows, out_rows, layer):
    bsz, _, nchunk, _ = xg.shape
    gb = S5_GROUP_BLOCK
    g0 = layer * (S5_GROUPS // gb)
    return pl.pallas_call(
        _s5_output_kernel,
        grid=(bsz, S5_GROUPS // gb),
        in_specs=[
            pl.BlockSpec((None, gb, nchunk, S5_ROW), lambda b, g: (b, g, 0, 0)),
            pl.BlockSpec((None, gb, nchunk, S5_HALF), lambda b, g: (b, g, 0, 0)),
            pl.BlockSpec((None, gb, nchunk, S5_HALF), lambda b, g: (b, g, 0, 0)),
            pl.BlockSpec((gb, 2, S5_GROUP, 2 * S5_ROW), lambda b, g: (g0 + g, 0, 0, 0)),
            pl.BlockSpec((gb, 4, S5_GROUP, S5_SLANES), lambda b, g: (g0 + g, 0, 0, 0)),
        ],
        out_specs=pl.BlockSpec((None, gb, nchunk, S5_ROW), lambda b, g: (b, g, 0, 0)),
        out_shape=jax.ShapeDtypeStruct((bsz, S5_GROUPS, nchunk, S5_ROW), BF16),
        scratch_shapes=[pltpu.VMEM((gb, S5_ROW, S5_ROW), BF16), pltpu.VMEM((gb, S5_ROW, S5_SLANES), BF16)],
        compiler_params=_cparams("arbitrary", "arbitrary"),
        name="s5_output",
    )(xg, hf, hb, toep_rows, out_rows)


def _s5_branch(xg_c, xg_l, mats, layer, need_ctx):
    toep_rows, out_rows, inj_rows, a1, a2 = mats
    bsz = xg_l.shape[0]
    a1r = jnp.tile(a1[layer], (1, bsz, 1))
    a2r = jnp.tile(a2[layer], (1, bsz, 1))
    sf_c, sb_c = _s5_inject(xg_c, inj_rows, layer)
    sf_l, sb_l = _s5_inject(xg_l, inj_rows, layer)
    hf_c, hb_c, h_ctx = _s5_scan(sf_c, sb_c, a1r, a2r, jnp.zeros_like(a1r))
    hf_l, hb_l, _ = _s5_scan(sf_l, sb_l, a1r, a2r, h_ctx)
    y_l = _s5_output(xg_l, hf_l, hb_l, toep_rows, out_rows, layer)
    y_c = _s5_output(xg_c, hf_c, hb_c, toep_rows, out_rows, layer) if need_ctx else None
    return y_c, y_l


def _gla_kernel(qf_ref, kf_ref, vf_ref, lrf_ref, qb_ref, kb_ref, vb_ref, lrb_ref, wg_ref, bg_ref, s0_ref,
                of_ref, ob_ref, sfin_ref, s_scr):
    w = pl.program_id(0)

    @pl.when(w == 0)
    def _():
        s_scr[...] = s0_ref[...]

    row = lax.broadcasted_iota(jnp.int32, (GLA_COL, GLA_COL), 0)
    col = lax.broadcasted_iota(jnp.int32, (GLA_COL, GLA_COL), 1)
    same_chunk = (row // GLA_CHUNK) == (col // GLA_CHUNK)
    r64 = lax.broadcasted_iota(jnp.int32, (GLA_CHUNK, GLA_CHUNK), 0)
    c64 = lax.broadcasted_iota(jnp.int32, (GLA_CHUNK, GLA_CHUNK), 1)
    dirs = (
        (qf_ref, kf_ref, vf_ref, lrf_ref, of_ref, jnp.where(same_chunk & (col <= row), 1.0, 0.0).astype(BF16),
         c64 <= r64),
        (qb_ref, kb_ref, vb_ref, lrb_ref, ob_ref, jnp.where(same_chunk & (col >= row), 1.0, 0.0).astype(BF16),
         c64 >= r64),
    )
    chains = [(b, d) for b in range(s_scr.shape[0]) for d in range(2)]

    pre = []
    for b, d in chains:
        q_ref, k_ref, _, lr_ref, _, tri, _ = dirs[d]
        z = jnp.dot(lr_ref[b].astype(BF16), wg_ref[d], preferred_element_type=F32) + bg_ref[d]
        g = (jnp.minimum(z, 0.0) - jnp.log1p(jnp.exp(-jnp.abs(z)))) * (1.0 / GLA_NORMALIZER)
        g_hi = g.astype(BF16)
        g_lo = (g - g_hi.astype(F32)).astype(BF16)
        gc = jnp.dot(tri, g_hi, preferred_element_type=F32) + jnp.dot(tri, g_lo, preferred_element_type=F32)
        qt = (q_ref[b] * jnp.exp(gc) * (GLA_DK ** -0.5)).astype(BF16)
        pre.append((qt, gc.T, k_ref[b].T))

    for pos in range(2):
        keys = []
        for (b, d), (qt, gct, kt) in zip(chains, pre):
            ci = pos if d == 0 else 1 - pos
            r0 = ci * GLA_CHUNK
            last = r0 + GLA_CHUNK - 1 if d == 0 else r0
            gct_c = gct[:, r0:r0 + GLA_CHUNK]
            dec = jnp.exp(gct[:, last:last + 1])
            k_in = kt[:, r0:r0 + GLA_CHUNK] * jnp.exp(-gct_c)
            keys.append((r0, dec, k_in.astype(BF16), (k_in * dec).astype(BF16)))
        for h in range(GLA_HEADS):
            ks = slice(h * GLA_DK, (h + 1) * GLA_DK)
            vs = slice(h * GLA_DV, (h + 1) * GLA_DV)
            scores = []
            for (b, d), (qt, _, _), (r0, _, k_in, _) in zip(chains, pre, keys):
                sc = jnp.dot(qt[r0:r0 + GLA_CHUNK, ks], k_in[ks, :], preferred_element_type=F32)
                scores.append(jnp.where(dirs[d][6], sc, 0.0).astype(BF16))
            for (b, d), (qt, _, _), (r0, dec, _, k_st), sc in zip(chains, pre, keys, scores):
                v_h = dirs[d][2][b, r0:r0 + GLA_CHUNK, vs].astype(BF16)
                s_h = s_scr[b, d, ks, :]
                o_h = jnp.dot(sc, v_h, preferred_element_type=F32)
                o_h = o_h + jnp.dot(qt[r0:r0 + GLA_CHUNK, ks], s_h.astype(BF16), preferred_element_type=F32)
                dirs[d][4][b, r0:r0 + GLA_CHUNK, vs] = o_h
                s_scr[b, d, ks, :] = dec[ks, :] * s_h + jnp.dot(k_st[ks, :], v_h, preferred_element_type=F32)

    @pl.when(w == pl.num_programs(0) - 1)
    def _():
        sfin_ref[...] = s_scr[...]


def _gla(q, k, v, lr, wg, bg, layer, s0):
    bsz, ncol = q.shape[:2]
    fwd = lambda w: (0, w, 0, 0)
    bwd = lambda w: (0, ncol - 1 - w, 0, 0)
    widths = (GLA_KEY, GLA_KEY, GLA_WIDTH, LR_PAD)
    state_spec = pl.BlockSpec((bsz, 2, GLA_KEY, GLA_DV), lambda w: (0, 0, 0, 0))
    in_specs = ([pl.BlockSpec((bsz, None, GLA_COL, wd), fwd) for wd in widths]
                + [pl.BlockSpec((bsz, None, GLA_COL, wd), bwd) for wd in widths]
                + [pl.BlockSpec((None, 2, LR_PAD, GLA_KEY), lambda w: (layer, 0, 0, 0)),
                   pl.BlockSpec((None, 2, 1, GLA_KEY), lambda w: (layer, 0, 0, 0)),
                   state_spec])
    o_shape = jax.ShapeDtypeStruct((bsz, ncol, GLA_COL, GLA_WIDTH), F32)
    return pl.pallas_call(
        _gla_kernel,
        grid=(ncol,),
        in_specs=in_specs,
        out_specs=[pl.BlockSpec((bsz, None, GLA_COL, GLA_WIDTH), fwd),
                   pl.BlockSpec((bsz, None, GLA_COL, GLA_WIDTH), bwd),
                   state_spec],
        out_shape=[o_shape, o_shape, jax.ShapeDtypeStruct((bsz, 2, GLA_KEY, GLA_DV), F32)],
        scratch_shapes=[pltpu.VMEM((bsz, 2, GLA_KEY, GLA_DV), F32)],
        compiler_params=_cparams("arbitrary"),
        name="gla",
    )(q, k, v, lr, q, k, v, lr, wg, bg, s0)


def _finish_kernel(ys_ref, zs_ref, of_ref, ob_ref, zg_ref, x_ref, mod_ref, wglu_ref, bglu_ref, gn_ref, wout_ref,
                   fin_ref, o_ref, ys_scr, o_scr, *, final, colmajor):
    nchunk = ys_ref.shape[1]
    for half in range(S5_CHUNK // SLOTS):
        for kt in range(S5_WIDTH // LANES):
            ws = [ys_ref[kt * SLOTS + qq, :, half * LANES:(half + 1) * LANES].astype(F32) for qq in range(SLOTS)]
            vs = _swap_slots_with_index(ws)
            for jj in range(SLOTS):
                ys_scr[kt, pl.ds(half * SLOTS + jj, nchunk, stride=CHUNK_PITCH), :] = vs[jj]
    ys = jnp.concatenate(
        [jnp.concatenate([ys_scr[kt, c * CHUNK_PITCH:c * CHUNK_PITCH + S5_CHUNK, :] for c in range(nchunk)], axis=0)
         for kt in range(S5_WIDTH // LANES)], axis=1)
    s = jax.nn.gelu(ys, approximate=True)
    t = jnp.dot(s.astype(BF16), wglu_ref[...], preferred_element_type=F32) + bglu_ref[...]
    zs = zs_ref[...].astype(F32)
    a_s = (s * jax.nn.sigmoid(t)) * (zs * jax.nn.sigmoid(zs))
    y = jnp.dot(a_s.astype(BF16), wout_ref[0:S5_WIDTH, :], preferred_element_type=F32)

    if colmajor:
        grid_rows = of_ref.shape[1]
        for w in range(GRID_W):
            o_w = of_ref[w] + ob_ref[w]
            for h in range(GLA_HEADS):
                o_scr[h, pl.ds(w, grid_rows, stride=GRID_ROW_PITCH), :] = o_w[:, h * GLA_DV:(h + 1) * GLA_DV]
        o_heads = [jnp.concatenate([o_scr[h, r * GRID_ROW_PITCH:r * GRID_ROW_PITCH + GRID_W, :]
                                    for r in range(grid_rows)], axis=0) for h in range(GLA_HEADS)]
    else:
        o_sum = of_ref[...] + ob_ref[...]
        o_heads = [o_sum[:, h * GLA_DV:(h + 1) * GLA_DV] for h in range(GLA_HEADS)]
    zg = zg_ref[...].astype(F32)
    gate_g = zg * jax.nn.sigmoid(zg)
    for h in range(GLA_HEADS):
        o_h = o_heads[h]
        o_n = o_h * lax.rsqrt(jnp.mean(o_h * o_h, axis=-1, keepdims=True) + EPS) * gn_ref[...]
        a_h = (o_n * gate_g[:, h * GLA_DV:(h + 1) * GLA_DV]).astype(BF16)
        y = y + jnp.dot(a_h, wout_ref[S5_WIDTH + h * GLA_DV:S5_WIDTH + (h + 1) * GLA_DV, :],
                        preferred_element_type=F32)
    gate = mod_ref[:, 2 * D_MODEL:3 * D_MODEL]
    xn = x_ref[...] + gate * y
    if final:
        ms = jnp.mean(xn * xn, axis=-1, keepdims=True)
        xn = xn * lax.rsqrt(ms + EPS) * fin_ref[...]
    o_ref[...] = xn


def _finish(ys, zs, o_f, o_b, zg, x2, mods, layer, ctx_mod_row, w_glu, b_glu, gnorm, w_out, fin_gain, final,
            colmajor):
    rows = x2.shape[0]
    per_seq = rows // ys.shape[0]
    tb = min(ROW_BLOCK, per_seq)
    bps = per_seq // tb
    mod_row = (lambda i: layer * 8 + i // bps) if ctx_mod_row is None else (lambda i: layer * 8 + ctx_mod_row)
    rb = lambda w: pl.BlockSpec((tb, w), lambda i: (i, 0))
    full = lambda a: pl.BlockSpec(a.shape, lambda i: (0,) * a.ndim)
    of_layer = lambda a: pl.BlockSpec((None,) + a.shape[1:], lambda i: (layer,) + (0,) * (a.ndim - 1))
    if colmajor:
        assert tb % GRID_W == 0
        o_spec = pl.BlockSpec((None, GRID_W, tb // GRID_W, GLA_WIDTH), lambda i: (i // bps, 0, i % bps, 0))
    else:
        o_spec = rb(GLA_WIDTH)
    return pl.pallas_call(
        functools.partial(_finish_kernel, final=final, colmajor=colmajor),
        grid=(rows // tb,),
        in_specs=[pl.BlockSpec((None, S5_GROUPS, tb // S5_CHUNK, S5_ROW), lambda i: (i // bps, 0, i % bps, 0)),
                  rb(S5_WIDTH), o_spec, o_spec, rb(GLA_WIDTH), rb(D_MODEL),
                  pl.BlockSpec((None, 1, 3 * D_MODEL), lambda i: (mod_row(i), 0, 0)),
                  of_layer(w_glu), of_layer(b_glu), of_layer(gnorm), of_layer(w_out), full(fin_gain)],
        out_specs=rb(D_MODEL),
        out_shape=jax.ShapeDtypeStruct((rows, D_MODEL), F32),
        scratch_shapes=[pltpu.VMEM((S5_WIDTH // LANES, tb // S5_CHUNK * CHUNK_PITCH, LANES), F32),
                        pltpu.VMEM((GLA_HEADS, max(tb // GRID_W, 1) * GRID_ROW_PITCH, GLA_DV), F32)],
        compiler_params=_cparams("arbitrary"),
        name="finish",
    )(ys, zs, o_f, o_b, zg, x2, mods, w_glu, b_glu, gnorm, w_out, fin_gain)


def _layer(x_lat, x_ctx, mods, sw, layer, need_ctx_out, final):
    bsz, l, _ = x_lat.shape
    lc = x_ctx.shape[1]
    lat_row = None
    ctx_row = bsz

    pj_l = _inproj(x_lat.reshape(bsz * l, D_MODEL), mods, sw["gain"], sw["w_in"], layer, lat_row, bsz, colmajor=True)
    pj_c = _inproj(x_ctx.reshape(bsz * lc, D_MODEL), mods, sw["gain"], sw["w_in"], layer, ctx_row, bsz,
                   colmajor=False)
    xg_l, zs_l, q_l, k_l, v_l, zg_l, lr_l = pj_l
    xg_c, zs_c, q_c, k_c, v_c, zg_c, lr_c = pj_c

    ys_c, ys_l = _s5_branch(xg_c, xg_l, sw["s5"], layer, need_ctx_out)

    cols = lambda t: t.reshape(bsz, lc // GLA_COL, GLA_COL, t.shape[-1])
    s_zero = jnp.zeros((bsz, 2, GLA_KEY, GLA_DV), F32)
    ocf, ocb, s_ctx = _gla(cols(q_c), cols(k_c), cols(v_c), cols(lr_c), sw["wg"], sw["bg"], layer, s_zero)
    olf, olb, _ = _gla(q_l, k_l, v_l, lr_l, sw["wg"], sw["bg"], layer, s_ctx)

    fin_args = (sw["w_glu"], sw["b_glu"], sw["gnorm"], sw["w_out"], sw["fin"])
    x_lat_new = _finish(ys_l, zs_l, olf, olb, zg_l, x_lat.reshape(bsz * l, D_MODEL), mods, layer, lat_row,
                        *fin_args, final=final, colmajor=True).reshape(bsz, l, D_MODEL)
    x_ctx_new = x_ctx
    if need_ctx_out:
        x_ctx_new = _finish(ys_c, zs_c, ocf.reshape(bsz * lc, GLA_WIDTH), ocb.reshape(bsz * lc, GLA_WIDTH), zg_c,
                            x_ctx.reshape(bsz * lc, D_MODEL), mods, layer, ctx_row, *fin_args, final=False,
                            colmajor=False).reshape(bsz, lc, D_MODEL)
    return x_lat_new, x_ctx_new


def _stack_weights(norm_g, w_in, s5_lam_re, s5_lam_im, s5_log_dt, s5_b_re, s5_b_im, s5_c_re, s5_c_im, s5_d,
                   s5_w_glu, s5_b_glu, gla_w_gate, gla_b_gate, gla_norm_g, w_out, final_norm):
    w_pad = jnp.pad(w_in, ((0, 0), (0, 0), (0, IN_PAD - w_in.shape[-1]))).astype(BF16)
    wg = jnp.zeros((DEPTH, 2, LR_PAD, GLA_KEY), F32)
    for d in range(2):
        wg = wg.at[:, d, d * GLA_RANK:(d + 1) * GLA_RANK].set(gla_w_gate[:, d])
    lag_rows, out_rows, inj_rows, a1, a2 = jax.vmap(_s5_matrices)(
        s5_lam_re, s5_lam_im, s5_log_dt, s5_b_re, s5_b_im, s5_c_re, s5_c_im, s5_d)
    merged = lambda t: t.reshape((DEPTH * S5_GROUPS,) + t.shape[2:])
    out_rows, inj_rows = merged(out_rows), merged(inj_rows)
    toep_rows = _s5_lags(merged(lag_rows), out_rows)
    return {
        "gain": norm_g.reshape(DEPTH, 1, D_MODEL),
        "w_in": w_pad,
        "s5": (toep_rows, out_rows, inj_rows, a1, a2),
        "wg": wg.astype(BF16),
        "bg": gla_b_gate.reshape(DEPTH, 2, 1, GLA_KEY),
        "w_glu": s5_w_glu.astype(BF16),
        "b_glu": s5_b_glu.reshape(DEPTH, 1, S5_WIDTH),
        "gnorm": gla_norm_g.reshape(DEPTH, 1, GLA_DV),
        "w_out": w_out.astype(BF16),
        "fin": final_norm.reshape(1, D_MODEL),
    }


def kernel(x, c, ctx, c_ctx, norm_g, w_mod, b_mod, w_in, s5_lam_re, s5_lam_im, s5_log_dt, s5_b_re, s5_b_im,
           s5_c_re, s5_c_im, s5_d, s5_w_glu, s5_b_glu, gla_w_gate, gla_b_gate, gla_norm_g, w_out, final_norm):
    bsz = x.shape[0]
    cond8 = jnp.concatenate([c, c_ctx[None], jnp.zeros((8 - bsz - 1, D_MODEL), F32)], axis=0)
    mods = _modulation(cond8, w_mod, b_mod).reshape(DEPTH * 8, 1, 3 * D_MODEL)
    sw = _stack_weights(norm_g, w_in, s5_lam_re, s5_lam_im, s5_log_dt, s5_b_re, s5_b_im, s5_c_re, s5_c_im,
                        s5_d, s5_w_glu, s5_b_glu, gla_w_gate, gla_b_gate, gla_norm_g, w_out, final_norm)
    x_lat, x_ctx = x, ctx
    for i in range(DEPTH):
        last = i == DEPTH - 1
        x_lat, x_ctx = _layer(x_lat, x_ctx, mods, sw, i, need_ctx_out=not last, final=last)
    return x_lat
```
